```python
import math
import jax, jax.numpy as jnp
from jax import lax
import numpy as np

D_MODEL = 1024
BATCH = 16
SEQ = 2048
DEPTH = 1

MIX_WIDTH = D_MODEL
ATTN_WIDTH = MIX_WIDTH // 2
POOL_WIDTH = MIX_WIDTH - ATTN_WIDTH
N_DIFF_HEADS = 4
DIFF_HEAD_DIM = ATTN_WIDTH // (2 * N_DIFF_HEADS)
DIFF_VALUE_DIM = 2 * DIFF_HEAD_DIM
ROT_DIM = DIFF_HEAD_DIM // 4
ROPE_THETA = 500000.0
Q_BLOCK = 128
POOL_WINDOWS = (2, 4, 8, 16)
N_POOL_GROUPS = len(POOL_WINDOWS)
POOL_GROUP_DIM = POOL_WIDTH // N_POOL_GROUPS
IN_COLS = 3 * ATTN_WIDTH + POOL_WIDTH
N_EXPERT_GROUPS = 4
EXPERTS_PER_GROUP = 8
N_EXPERTS = N_EXPERT_GROUPS * EXPERTS_PER_GROUP
TOP_K_IN_GROUP = 2
D_EXPERT = D_MODEL // 2
MOE_BLOCK = 128
RMS_EPS = 1e-6
N_MOD = 6

kernel_name = "hybrid_diffattn_pool_hmoe_encoder"


def rms_norm(x, g):
    xf = x.astype(jnp.float32)
    y = xf * lax.rsqrt(jnp.mean(xf * xf, axis=-1, keepdims=True) + RMS_EPS)
    return (y * g.astype(jnp.float32)).astype(x.dtype)


def lambda_init_for(layer_idx):
    return 0.8 - 0.6 * math.exp(-0.3 * layer_idx)


def rope_tables(seq_len):
    pos = jnp.arange(seq_len, dtype=jnp.float32)
    inv_freq = ROPE_THETA ** (-jnp.arange(0, ROT_DIM, 2, dtype=jnp.float32) / ROT_DIM)
    ang = pos[:, None] * inv_freq[None, :]
    return jnp.cos(ang), jnp.sin(ang)


def apply_partial_rope(t, cos, sin):
    half = ROT_DIM // 2
    cs = cos[None, :, None, None, :].astype(t.dtype)
    sn = sin[None, :, None, None, :].astype(t.dtype)
    t1 = t[..., :half]
    t2 = t[..., half:ROT_DIM]
    return jnp.concatenate([t1 * cs - t2 * sn, t2 * cs + t1 * sn, t[..., ROT_DIM:]], axis=-1)


def diff_attention(q, k, v, lam, subln_g, lambda_init):
    B, S = q.shape[0], q.shape[1]
    cos, sin = rope_tables(S)
    q = apply_partial_rope(q, cos, sin) * (DIFF_HEAD_DIM ** -0.5)
    k = apply_partial_rope(k, cos, sin)
    nqb = S // Q_BLOCK
    qb = q.reshape(B, nqb, Q_BLOCK, N_DIFF_HEADS, 2, DIFF_HEAD_DIM).transpose(1, 0, 3, 4, 2, 5)
    kt = k.transpose(0, 2, 3, 1, 4)
    vt = v.transpose(0, 2, 1, 3)

    def block(qblk):
        s = jnp.einsum('bhcqd,bhckd->bhcqk', qblk, kt).astype(jnp.float32)
        a = jax.nn.softmax(s, axis=-1)
        w = a[:, :, 0] - lam * a[:, :, 1]
        return jnp.einsum('bhqk,bhkv->bhqv', w.astype(vt.dtype), vt)

    o = lax.map(block, qb)
    o = o.transpose(1, 0, 3, 2, 4).reshape(B, S, N_DIFF_HEADS, DIFF_VALUE_DIM)
    o = rms_norm(o, subln_g) * (1.0 - lambda_init)
    return o.reshape(B, S, N_DIFF_HEADS * DIFF_VALUE_DIM)


def multiscale_pool(u, w_pool, pool_scale):
    B, S = u.shape[0], u.shape[1]
    uf = u.astype(jnp.float32)
    cs = jnp.concatenate([jnp.zeros_like(uf[:, :1]), jnp.cumsum(uf, axis=1)], axis=1)
    pos = jnp.arange(S)
    outs = []
    for g, w in enumerate(POOL_WINDOWS):
        left = w // 2
        right = w - 1 - left
        lo = jnp.clip(pos - left, 0, S)
        hi = jnp.clip(pos + right + 1, 0, S)
        cnt = (hi - lo).astype(jnp.float32)
        mean = (cs[:, hi, g] - cs[:, lo, g]) / cnt[None, :, None]
        outs.append(mean - uf[:, :, g])
    d = jnp.stack(outs, axis=2).astype(u.dtype)
    y = jnp.einsum('bsgc,gcd->bsgd', d, w_pool) * pool_scale
    return y.reshape(B, S, N_POOL_GROUPS * POOL_GROUP_DIM)


def grouped_expert_ffn(xt, flat_e, flat_tok, flat_w, w_gate, w_up, w_down):
    N, D = xt.shape
    NK = flat_e.shape[0]
    order = jnp.argsort(flat_e)
    sorted_e = flat_e[order]
    counts = jax.ops.segment_sum(jnp.ones((NK,), jnp.int32), flat_e, num_segments=N_EXPERTS)
    starts = jnp.cumsum(counts) - counts
    padded = ((counts + MOE_BLOCK - 1) // MOE_BLOCK) * MOE_BLOCK
    pad_ends = jnp.cumsum(padded)
    pad_starts = pad_ends - padded
    rank = jnp.arange(NK, dtype=jnp.int32) - starts[sorted_e]
    dest = pad_starts[sorted_e] + rank
    n_blocks = -(-NK // MOE_BLOCK) + N_EXPERTS
    P = n_blocks * MOE_BLOCK
    row_tok = jnp.full((P,), N, jnp.int32).at[dest].set(flat_tok[order])
    row_w = jnp.zeros((P,), jnp.float32).at[dest].set(flat_w[order])
    block_e = jnp.minimum(
        jnp.searchsorted(pad_ends, jnp.arange(n_blocks, dtype=jnp.int32) * MOE_BLOCK, side='right'),
        N_EXPERTS - 1).astype(jnp.int32)
    x_pad = jnp.concatenate([xt, jnp.zeros((1, D), xt.dtype)], axis=0)
    xb = x_pad[row_tok].reshape(n_blocks, MOE_BLOCK, D)

    def expert_block(args):
        xblk, e = args
        hid = jax.nn.silu(xblk @ w_gate[e]) * (xblk @ w_up[e])
        return hid @ w_down[e]

    yb = lax.map(expert_block, (xb, block_e)).reshape(P, D)
    y = jnp.zeros((N + 1, D), jnp.float32).at[row_tok].add(yb.astype(jnp.float32) * row_w[:, None])
    return y[:N].astype(xt.dtype)


def hierarchical_moe(h, w_rg, b_rg, w_re, b_re, w_gate, w_up, w_down):
    B, S, D = h.shape
    N = B * S
    xt = h.reshape(N, D)
    g_logits = (xt @ w_rg).astype(jnp.float32) + b_rg.astype(jnp.float32)
    g_prob = jax.nn.softmax(g_logits, axis=-1)
    _, g_idx = lax.top_k(g_logits, 1)
    g_sel = g_idx[:, 0]
    tok = jnp.arange(N)
    g_p = g_prob[tok, g_sel]
    e_logits = ((xt @ w_re).astype(jnp.float32) + b_re.astype(jnp.float32)).reshape(N, N_EXPERT_GROUPS, EXPERTS_PER_GROUP)
    e_in = e_logits[tok, g_sel]
    e_prob = jax.nn.softmax(e_in, axis=-1)
    top_p, top_i = lax.top_k(e_prob, TOP_K_IN_GROUP)
    top_p = top_p / jnp.sum(top_p, axis=-1, keepdims=True)
    weights = g_p[:, None] * top_p
    expert_id = (g_sel[:, None] * EXPERTS_PER_GROUP + top_i).astype(jnp.int32)
    flat_tok = jnp.repeat(jnp.arange(N, dtype=jnp.int32), TOP_K_IN_GROUP)
    y = grouped_expert_ffn(xt, expert_id.reshape(-1), flat_tok, weights.reshape(-1), w_gate, w_up, w_down)
    return y.reshape(B, S, D)


def setup_inputs(seed: int = 0) -> dict:
    key = jax.random.key(seed)
    ks = jax.random.split(key, 24)
    f32 = jnp.float32
    D = D_MODEL
    nrm = lambda k, shape, s: jax.random.normal(k, shape, f32) * s
    return {
        "x": nrm(ks[0], (BATCH, SEQ, D), 1.0),
        "c": nrm(ks[1], (BATCH, D), 1.0),
        "w_ada": nrm(ks[2], (DEPTH, D, N_MOD * D), 0.3 * D ** -0.5),
        "b_ada": nrm(ks[3], (DEPTH, N_MOD * D), 0.02),
        "norm1_g": 1.0 + nrm(ks[4], (DEPTH, D), 0.02),
        "w_in": nrm(ks[5], (DEPTH, D, IN_COLS), D ** -0.5),
        "lambda_q1": nrm(ks[6], (DEPTH, DIFF_HEAD_DIM), 0.1),
        "lambda_k1": nrm(ks[7], (DEPTH, DIFF_HEAD_DIM), 0.1),
        "lambda_q2": nrm(ks[8], (DEPTH, DIFF_HEAD_DIM), 0.1),
        "lambda_k2": nrm(ks[9], (DEPTH, DIFF_HEAD_DIM), 0.1),
        "subln_g": 1.0 + nrm(ks[10], (DEPTH, DIFF_VALUE_DIM), 0.02),
        "w_pool": nrm(ks[11], (DEPTH, N_POOL_GROUPS, POOL_GROUP_DIM, POOL_GROUP_DIM), POOL_GROUP_DIM ** -0.5),
        "pool_scale": 1.0 + nrm(ks[12], (DEPTH, N_POOL_GROUPS, POOL_GROUP_DIM), 0.1),
        "w_out": nrm(ks[13], (DEPTH, MIX_WIDTH, D), MIX_WIDTH ** -0.5),
        "norm2_g": 1.0 + nrm(ks[14], (DEPTH, D), 0.02),
        "w_router_group": nrm(ks[15], (DEPTH, D, N_EXPERT_GROUPS), D ** -0.5),
        "b_router_group": nrm(ks[16], (DEPTH, N_EXPERT_GROUPS), 0.01),
        "w_router_expert": nrm(ks[17], (DEPTH, D, N_EXPERTS), D ** -0.5),
        "b_router_expert": nrm(ks[18], (DEPTH, N_EXPERTS), 0.01),
        "w_gate": nrm(ks[19], (DEPTH, N_EXPERTS, D, D_EXPERT), D ** -0.5),
        "w_up": nrm(ks[20], (DEPTH, N_EXPERTS, D, D_EXPERT), D ** -0.5),
        "w_down": nrm(ks[21], (DEPTH, N_EXPERTS, D_EXPERT, D), D_EXPERT ** -0.5),
        "final_g": 1.0 + nrm(ks[22], (D,), 0.02),
    }


def reference(x, c, w_ada, b_ada, norm1_g, w_in, lambda_q1, lambda_k1, lambda_q2, lambda_k2,
              subln_g, w_pool, pool_scale, w_out, norm2_g, w_router_group, b_router_group,
              w_router_expert, b_router_expert, w_gate, w_up, w_down, final_g):
    B, S, D = x.shape
    A = ATTN_WIDTH
    for l in range(DEPTH):
        lambda_init = lambda_init_for(l)
        mod = jax.nn.silu(c) @ w_ada[l] + b_ada[l]
        sh1, sc1, g1, sh2, sc2, g2 = jnp.split(mod, N_MOD, axis=-1)
        h = rms_norm(x, norm1_g[l]) * (1.0 + sc1[:, None]) + sh1[:, None]
        z = h @ w_in[l]
        q = z[..., :A].reshape(B, S, N_DIFF_HEADS, 2, DIFF_HEAD_DIM)
        k = z[..., A:2 * A].reshape(B, S, N_DIFF_HEADS, 2, DIFF_HEAD_DIM)
        v = z[..., 2 * A:3 * A].reshape(B, S, N_DIFF_HEADS, DIFF_VALUE_DIM)
        u = z[..., 3 * A:].reshape(B, S, N_POOL_GROUPS, POOL_GROUP_DIM)
        lam = (jnp.exp(jnp.sum(lambda_q1[l].astype(jnp.float32) * lambda_k1[l].astype(jnp.float32)))
               - jnp.exp(jnp.sum(lambda_q2[l].astype(jnp.float32) * lambda_k2[l].astype(jnp.float32)))
               + lambda_init)
        a_out = diff_attention(q, k, v, lam, subln_g[l], lambda_init)
        p_out = multiscale_pool(u, w_pool[l], pool_scale[l])
        mix = jnp.concatenate([a_out, p_out], axis=-1) @ w_out[l]
        x = x + g1[:, None] * mix
        h = rms_norm(x, norm2_g[l]) * (1.0 + sc2[:, None]) + sh2[:, None]
        y = hierarchical_moe(h, w_router_group[l], b_router_group[l], w_router_expert[l],
                             b_router_expert[l], w_gate[l], w_up[l], w_down[l])
        x = x + g2[:, None] * y
    return rms_norm(x, final_g)
```

```python
import functools
import math

import jax
import jax.numpy as jnp
from jax import lax
from jax.experimental import pallas as pl
from jax.experimental.pallas import tpu as pltpu

F32 = jnp.float32
BF16 = jnp.bfloat16
I32 = jnp.int32
HIGHEST = lax.Precision.HIGHEST

D_MODEL = 1024
ATTN_WIDTH = 512
N_HEADS = 4
HEAD_DIM = 64
VALUE_DIM = 128
ROT_DIM = 16
ROPE_THETA = 500000.0
POOL_WINDOWS = (2, 4, 8, 16)
POOL_GROUP_DIM = 128
N_GROUPS = 4
EXPERTS_PER_GROUP = 8
N_EXPERTS = 32
D_EXPERT = 512
RMS_EPS = 1e-6
LAMBDA_INIT = 0.8 - 0.6 * math.exp(-0.3 * 0)

LANES = 128
SUBLANES = 8
ROW_CHUNKS = D_MODEL // LANES

POOL_HALO = 8
FFN_BLOCK = 256
VMEM_LIMIT = 56 * 1024 * 1024


def _cparams(sem, vmem=VMEM_LIMIT):
    return pltpu.CompilerParams(dimension_semantics=sem, vmem_limit_bytes=vmem)


def _ada_kernel(c_ref, w_ref, b_ref, o_ref):
    c = c_ref[...]
    act = c * jax.nn.sigmoid(c)
    o_ref[...] = jnp.dot(act, w_ref[...], precision=HIGHEST, preferred_element_type=F32) + b_ref[...]


def _ada(c, w_ada, b_ada):
    bsz = c.shape[0]
    n_mod = w_ada.shape[1] // D_MODEL
    return pl.pallas_call(
        _ada_kernel,
        grid=(n_mod,),
        in_specs=[
            pl.BlockSpec((bsz, D_MODEL), lambda j: (0, 0)),
            pl.BlockSpec((D_MODEL, D_MODEL), lambda j: (0, j)),
            pl.BlockSpec((1, D_MODEL), lambda j: (0, j)),
        ],
        out_specs=pl.BlockSpec((None, bsz, D_MODEL), lambda j: (j, 0, 0)),
        out_shape=jax.ShapeDtypeStruct((n_mod, bsz, D_MODEL), F32),
        compiler_params=_cparams(("arbitrary",)),
    )(c, w_ada, b_ada.reshape(1, -1))


def _inproj_kernel(x_ref, sh_ref, sc_ref, g_ref, w_ref, cos_ref, sa_ref, sb_ref,
                   q_ref, k_ref, v_ref, u_ref):
    x = x_ref[...]
    y = x * lax.rsqrt(jnp.mean(x * x, axis=-1, keepdims=True) + RMS_EPS) * g_ref[...]
    h = (y * (1.0 + sc_ref[...]) + sh_ref[...]).astype(BF16)
    cos, sa, sb = cos_ref[...], sa_ref[...], sb_ref[...]
    for part, o_ref, scale in ((0, q_ref, HEAD_DIM ** -0.5), (1, k_ref, None)):
        z = jnp.dot(h, w_ref[:, part * ATTN_WIDTH:(part + 1) * ATTN_WIDTH], preferred_element_type=F32)
        for hd in range(N_HEADS):
            zc = z[:, hd * LANES:(hd + 1) * LANES]
            r = zc * cos + pltpu.roll(zc, ROT_DIM // 2, 1) * sa + pltpu.roll(zc, LANES - ROT_DIM // 2, 1) * sb
            if scale is not None:
                r = r * scale
            o_ref[:, hd * LANES:(hd + 1) * LANES] = r.astype(BF16)
    v_ref[...] = jnp.dot(h, w_ref[:, 2 * ATTN_WIDTH:3 * ATTN_WIDTH], preferred_element_type=F32).astype(BF16)
    u_ref[...] = jnp.dot(h, w_ref[:, 3 * ATTN_WIDTH:], preferred_element_type=F32).astype(BF16)


def _rope_lane_tables(seq):
    pos = jnp.arange(seq, dtype=F32)
    inv_freq = ROPE_THETA ** (-jnp.arange(0, ROT_DIM, 2, dtype=F32) / ROT_DIM)
    ang = pos[:, None] * inv_freq[None, :]
    cos, sin = jnp.cos(ang), jnp.sin(ang)
    half = ROT_DIM // 2
    ones = jnp.ones((seq, HEAD_DIM - ROT_DIM), F32)
    zeros_h = jnp.zeros((seq, half), F32)
    zeros_r = jnp.zeros((seq, HEAD_DIM - ROT_DIM), F32)
    cos_t = jnp.concatenate([cos, cos, ones], axis=1)
    sa_t = jnp.concatenate([zeros_h, sin, zeros_r], axis=1)
    sb_t = jnp.concatenate([-sin, zeros_h, zeros_r], axis=1)
    rep = LANES // HEAD_DIM
    return jnp.tile(cos_t, (1, rep)), jnp.tile(sa_t, (1, rep)), jnp.tile(sb_t, (1, rep))


def _inproj(x2, mod4, norm1_g, w_in_bf, seq, tm=512):
    n = x2.shape[0]
    spt = seq // tm
    cos_t, sa_t, sb_t = _rope_lane_tables(seq)
    mod_spec = lambda which: pl.BlockSpec((None, None, 1, D_MODEL), lambda i: (which, i // spt, 0, 0))
    tab_spec = pl.BlockSpec((tm, LANES), lambda i: (i % spt, 0))
    out_spec = pl.BlockSpec((tm, ATTN_WIDTH), lambda i: (i, 0))
    out_sds = jax.ShapeDtypeStruct((n, ATTN_WIDTH), BF16)
    return pl.pallas_call(
        _inproj_kernel,
        grid=(n // tm,),
        in_specs=[
            pl.BlockSpec((tm, D_MODEL), lambda i: (i, 0)),
            mod_spec(0), mod_spec(1),
            pl.BlockSpec((1, D_MODEL), lambda i: (0, 0)),
            pl.BlockSpec((D_MODEL, 4 * ATTN_WIDTH), lambda i: (0, 0)),
            tab_spec, tab_spec, tab_spec,
        ],
        out_specs=[out_spec] * 4,
        out_shape=[out_sds] * 4,
        compiler_params=_cparams(("arbitrary",)),
    )(x2, mod4, mod4, norm1_g, w_in_bf, cos_t, sa_t, sb_t)


def _attn_kernel(lq1_ref, lk1_ref, lq2_ref, lk2_ref, sg_ref, q_ref, k_ref, v_ref, o_ref):
    lam = (jnp.exp(jnp.sum(lq1_ref[...] * lk1_ref[...], axis=-1, keepdims=True))
           - jnp.exp(jnp.sum(lq2_ref[...] * lk2_ref[...], axis=-1, keepdims=True)) + LAMBDA_INIT)
    q = q_ref[...]
    tq = q.shape[0]
    lane = lax.broadcasted_iota(I32, q.shape, 1)
    zero = jnp.zeros_like(q)
    qq = jnp.concatenate([jnp.where(lane < HEAD_DIM, q, zero), jnp.where(lane >= HEAD_DIM, q, zero)], axis=0)
    s = lax.dot_general(qq, k_ref[...], (((1,), (1,)), ((), ())), preferred_element_type=F32)
    p = jnp.exp(s - jnp.max(s, axis=-1, keepdims=True))
    inv = 1.0 / jnp.sum(p, axis=-1, keepdims=True)
    w = p[:tq] * inv[:tq] - p[tq:] * (lam * inv[tq:])
    o = jnp.dot(w.astype(BF16), v_ref[...], preferred_element_type=F32)
    o = o * lax.rsqrt(jnp.mean(o * o, axis=-1, keepdims=True) + RMS_EPS) * sg_ref[...]
    o_ref[...] = (o * (1.0 - LAMBDA_INIT)).astype(BF16)


def _attention(q, k, v, lq1, lk1, lq2, lk2, subln_g, bsz, seq, tq=256):
    n = q.shape[0]
    qpt = seq // tq
    vec_spec = lambda width: pl.BlockSpec((1, width), lambda b, h, j: (0, 0))
    kv_spec = pl.BlockSpec((seq, LANES), lambda b, h, j: (b, h))
    qo_spec = pl.BlockSpec((tq, LANES), lambda b, h, j: (b * qpt + j, h))
    return pl.pallas_call(
        _attn_kernel,
        grid=(bsz, N_HEADS, qpt),
        in_specs=[vec_spec(HEAD_DIM)] * 4 + [vec_spec(VALUE_DIM), qo_spec, kv_spec, kv_spec],
        out_specs=qo_spec,
        out_shape=jax.ShapeDtypeStruct((n, ATTN_WIDTH), BF16),
        compiler_params=_cparams(("arbitrary",) * 3),
    )(lq1, lk1, lq2, lk2, subln_g, q, k, v)


def _mix_kernel(a_ref, u_ref, x_ref, g1_ref, sh2_ref, sc2_ref, wp_ref, ps_ref, wo_ref, n2_ref,
                wr_ref, br_ref, x1_ref, h2_ref, eid_ref, wts_ref, up_ref, *, seq):
    j = pl.program_id(1)
    tm = x_ref.shape[0]
    pool_w = u_ref.shape[1]

    @pl.when(j == 0)
    def _():
        up_ref[0:POOL_HALO, :] = jnp.zeros((POOL_HALO, pool_w), F32)
        up_ref[POOL_HALO + seq:, :] = jnp.zeros((POOL_HALO, pool_w), F32)
        up_ref[POOL_HALO:POOL_HALO + seq, :] = u_ref[...].astype(F32)

    r0 = pl.multiple_of(j * tm, tm)
    pos = r0 + lax.broadcasted_iota(I32, (tm, 1), 0)
    mix = jnp.dot(a_ref[...], wo_ref[0:ATTN_WIDTH, :], preferred_element_type=F32)
    for g, win in enumerate(POOL_WINDOWS):
        left = win // 2
        right = win - 1 - left
        cols = slice(g * POOL_GROUP_DIM, (g + 1) * POOL_GROUP_DIM)
        win_rows = up_ref[pl.ds(r0, tm + 2 * POOL_HALO), cols]
        tot, span = win_rows, 1
        while span < win:
            tot = tot[:tot.shape[0] - span] + tot[span:]
            span *= 2
        tot = tot[POOL_HALO - left:POOL_HALO - left + tm]
        cnt = (jnp.minimum(pos + (right + 1), seq) - jnp.maximum(pos - left, 0)).astype(F32)
        d = tot / cnt - win_rows[POOL_HALO:POOL_HALO + tm]
        yg = jnp.dot(d.astype(BF16), wp_ref[g], preferred_element_type=F32) * ps_ref[g]
        mix = mix + jnp.dot(yg.astype(BF16), wo_ref[ATTN_WIDTH + g * POOL_GROUP_DIM:
                                                    ATTN_WIDTH + (g + 1) * POOL_GROUP_DIM, :],
                            preferred_element_type=F32)
    x1 = x_ref[...] + g1_ref[...] * mix
    x1_ref[...] = x1
    y = x1 * lax.rsqrt(jnp.mean(x1 * x1, axis=-1, keepdims=True) + RMS_EPS) * n2_ref[...]
    h2 = y * (1.0 + sc2_ref[...]) + sh2_ref[...]
    for c in range(ROW_CHUNKS):
        h2_ref[pl.ds(c, tm, stride=ROW_CHUNKS), :] = h2[:, c * LANES:(c + 1) * LANES]

    lt = lax.dot_general(wr_ref[...], h2, (((1,), (1,)), ((), ())), precision=HIGHEST,
                         preferred_element_type=F32) + br_ref[:, 0:1]
    gl = [lt[N_EXPERTS + i:N_EXPERTS + i + 1, :] for i in range(N_GROUPS)]
    gmax = jnp.maximum(jnp.maximum(gl[0], gl[1]), jnp.maximum(gl[2], gl[3]))
    gsel = jnp.where(gl[0] == gmax, 0, jnp.where(gl[1] == gmax, 1, jnp.where(gl[2] == gmax, 2, 3)))
    gden = sum(jnp.exp(g_ - gmax) for g_ in gl)
    e_in = lt[3 * EXPERTS_PER_GROUP:4 * EXPERTS_PER_GROUP, :]
    for i in (2, 1, 0):
        e_in = jnp.where(gsel == i, lt[i * EXPERTS_PER_GROUP:(i + 1) * EXPERTS_PER_GROUP, :], e_in)
    sub = lax.broadcasted_iota(I32, e_in.shape, 0)
    m1 = jnp.max(e_in, axis=0, keepdims=True)
    i1 = jnp.min(jnp.where(e_in == m1, sub, EXPERTS_PER_GROUP), axis=0, keepdims=True)
    rest = jnp.where(sub == i1, -jnp.inf, e_in)
    m2 = jnp.max(rest, axis=0, keepdims=True)
    i2 = jnp.min(jnp.where(rest == m2, sub, EXPERTS_PER_GROUP), axis=0, keepdims=True)
    p2 = jnp.exp(m2 - m1)
    w1 = 1.0 / ((1.0 + p2) * gden)
    w2 = p2 * w1
    row = lax.broadcasted_iota(I32, (SUBLANES, tm), 0)
    base = gsel * EXPERTS_PER_GROUP
    eid_ref[...] = jnp.where(row == 0, base + i1, jnp.where(row == 1, base + i2, 0))
    wts_ref[...] = jnp.where(row == 0, w1, jnp.where(row == 1, w2, 0.0))


def _mix(a_out, u, x2, mod4, w_pool_bf, pool_scale, w_out_bf, norm2_g, wr, br, bsz, seq, tm=512):
    n = x2.shape[0]
    spt = seq // tm
    mod_spec = lambda which: pl.BlockSpec((None, None, 1, D_MODEL), lambda b, j: (which, b, 0, 0))
    tok_spec = lambda width: pl.BlockSpec((tm, width), lambda b, j: (b * spt + j, 0))
    lane_spec = pl.BlockSpec((SUBLANES, tm), lambda b, j: (0, b * spt + j))
    n_rt = wr.shape[0]
    return pl.pallas_call(
        functools.partial(_mix_kernel, seq=seq),
        grid=(bsz, spt),
        in_specs=[
            tok_spec(ATTN_WIDTH),
            pl.BlockSpec((seq, ATTN_WIDTH), lambda b, j: (b, 0)),
            tok_spec(D_MODEL),
            mod_spec(2), mod_spec(3), mod_spec(4),
            pl.BlockSpec((len(POOL_WINDOWS), POOL_GROUP_DIM, POOL_GROUP_DIM), lambda b, j: (0, 0, 0)),
            pl.BlockSpec((len(POOL_WINDOWS), 1, POOL_GROUP_DIM), lambda b, j: (0, 0, 0)),
            pl.BlockSpec((D_MODEL, D_MODEL), lambda b, j: (0, 0)),
            pl.BlockSpec((1, D_MODEL), lambda b, j: (0, 0)),
            pl.BlockSpec((n_rt, D_MODEL), lambda b, j: (0, 0)),
            pl.BlockSpec((n_rt, LANES), lambda b, j: (0, 0)),
        ],
        out_specs=[
            tok_spec(D_MODEL),
            pl.BlockSpec((tm * ROW_CHUNKS, LANES), lambda b, j: (b * spt + j, 0)),
            lane_spec, lane_spec,
        ],
        out_shape=[
            jax.ShapeDtypeStruct((n, D_MODEL), F32),
            jax.ShapeDtypeStruct((n * ROW_CHUNKS, LANES), F32),
            jax.ShapeDtypeStruct((SUBLANES, n), I32),
            jax.ShapeDtypeStruct((SUBLANES, n), F32),
        ],
        scratch_shapes=[pltpu.VMEM((seq + 2 * POOL_HALO, ATTN_WIDTH), F32)],
        compiler_params=_cparams(("arbitrary", "arbitrary")),
    )(a_out, u, x2, mod4, mod4, mod4, w_pool_bf, pool_scale, w_out_bf, norm2_g, wr, br)


META_BLOCK_E, META_N_USED, META_PAD_END, META_COUNT = 0, 1, 2, 3


def _plan_kernel(eid_ref, dest_ref, meta_ref, cnt_ref, base_ref, *, n_blocks, meta_w):
    phase = pl.program_id(0)
    t = pl.program_id(1)
    nt = pl.num_programs(1)
    tt = eid_ref.shape[1]
    sub = lax.broadcasted_iota(I32, (N_EXPERTS, tt), 0)
    oh0 = sub == eid_ref[0:1, :]
    oh1 = sub == eid_ref[1:2, :]
    oh = jnp.where(oh0 | oh1, 1.0, 0.0)

    @pl.when((phase == 0) & (t == 0))
    def _():
        cnt_ref[...] = jnp.zeros_like(cnt_ref)

    @pl.when(phase == 0)
    def _():
        cnt_ref[...] += jnp.sum(oh, axis=1, keepdims=True)

    @pl.when((phase == 0) & (t == nt - 1))
    def _():
        cnt = cnt_ref[...]
        padded = jnp.floor((cnt + (FFN_BLOCK - 1)) * (1.0 / FFN_BLOCK)) * FFN_BLOCK
        er = lax.broadcasted_iota(I32, (N_EXPERTS, N_EXPERTS), 0)
        ec = lax.broadcasted_iota(I32, (N_EXPERTS, N_EXPERTS), 1)
        tri = jnp.where(ec <= er, 1.0, 0.0)
        pad_end = jnp.dot(tri, padded, precision=HIGHEST, preferred_element_type=F32)
        base_ref[...] = pad_end - padded
        blk_row = (lax.broadcasted_iota(I32, (N_EXPERTS, meta_w), 1) * FFN_BLOCK).astype(F32)
        block_e = jnp.sum(jnp.where(pad_end[:, 0:1] <= blk_row, 1.0, 0.0), axis=0, keepdims=True)
        block_e = jnp.minimum(block_e, N_EXPERTS - 1.0)
        n_used = pad_end[N_EXPERTS - 1:N_EXPERTS, 0:1] * (1.0 / FFN_BLOCK)
        diag = (lax.broadcasted_iota(I32, (N_EXPERTS, LANES), 0)
                == lax.broadcasted_iota(I32, (N_EXPERTS, LANES), 1))
        pe_lane = jnp.sum(jnp.where(diag, pad_end, 0.0), axis=0, keepdims=True)
        cnt_lane = jnp.sum(jnp.where(diag, cnt, 0.0), axis=0, keepdims=True)
        zpad = jnp.zeros((1, meta_w - LANES), F32)
        row = lax.broadcasted_iota(I32, (SUBLANES, meta_w), 0)
        meta = jnp.where(row == META_BLOCK_E, block_e,
                         jnp.where(row == META_N_USED, n_used,
                                   jnp.where(row == META_PAD_END, jnp.concatenate([pe_lane, zpad], axis=1),
                                             jnp.where(row == META_COUNT, jnp.concatenate([cnt_lane, zpad], axis=1),
                                                       0.0))))
        meta_ref[...] = meta.astype(I32)

    @pl.when(phase == 1)
    def _():
        r = lax.broadcasted_iota(I32, (tt, tt), 0)
        c = lax.broadcasted_iota(I32, (tt, tt), 1)
        triu = jnp.where(r <= c, 1.0, 0.0).astype(BF16)
        incl = jnp.dot(oh.astype(BF16), triu, preferred_element_type=F32)
        slot = base_ref[:, 0:1] + incl - oh
        d0 = jnp.sum(jnp.where(oh0, slot, 0.0), axis=0, keepdims=True)
        d1 = jnp.sum(jnp.where(oh1, slot, 0.0), axis=0, keepdims=True)
        row = lax.broadcasted_iota(I32, (SUBLANES, tt), 0)
        dest_ref[...] = jnp.where(row == 0, d0, jnp.where(row == 1, d1, 0.0)).astype(I32)
        base_ref[...] += incl[:, tt - 1:tt]


def _plan(eid, n_blocks, tt=512):
    n = eid.shape[1]
    meta_w = pl.cdiv(n_blocks, LANES) * LANES
    return pl.pallas_call(
        functools.partial(_plan_kernel, n_blocks=n_blocks, meta_w=meta_w),
        grid=(2, n // tt),
        in_specs=[pl.BlockSpec((SUBLANES, tt), lambda p, t: (0, t))],
        out_specs=[
            pl.BlockSpec((SUBLANES, tt), lambda p, t: (0, p * t)),
            pl.BlockSpec((SUBLANES, meta_w), lambda p, t: (0, 0)),
        ],
        out_shape=[
            jax.ShapeDtypeStruct((SUBLANES, n), I32),
            jax.ShapeDtypeStruct((SUBLANES, meta_w), I32),
        ],
        scratch_shapes=[pltpu.VMEM((N_EXPERTS, LANES), F32), pltpu.VMEM((N_EXPERTS, LANES), F32)],
        compiler_params=_cparams(("arbitrary", "arbitrary")),
    )(eid)


def _dispatch_kernel(meta_ref, dest_ref, h2_hbm, xs_hbm, zero_ref, zsem, sem, *, n_blocks):
    i = pl.program_id(0)
    tt = dest_ref.shape[1]
    blk_rows = FFN_BLOCK * ROW_CHUNKS

    def zero_copy(blk):
        start = pl.multiple_of(blk * blk_rows, blk_rows)
        return pltpu.make_async_copy(zero_ref, xs_hbm.at[pl.ds(start, blk_rows)], zsem)

    @pl.when(i == 0)
    def _():
        zero_ref[...] = jnp.zeros_like(zero_ref)
        for e in range(N_EXPERTS):
            @pl.when(meta_ref[META_COUNT, e] > 0)
            def _():
                blk = meta_ref[META_PAD_END, e] // FFN_BLOCK - 1
                cp = zero_copy(blk)
                cp.start()
                cp.wait()

        def tail(blk, carry):
            cp = zero_copy(blk)
            cp.start()
            cp.wait()
            return carry
        lax.fori_loop(meta_ref[META_N_USED, 0], n_blocks, tail, 0)

    def row_copy(tok, slot):
        src = pl.multiple_of(tok * ROW_CHUNKS, ROW_CHUNKS)
        dst = pl.multiple_of(slot * ROW_CHUNKS, ROW_CHUNKS)
        return pltpu.make_async_copy(h2_hbm.at[pl.ds(src, ROW_CHUNKS)], xs_hbm.at[pl.ds(dst, ROW_CHUNKS)], sem)

    def issue(t, carry):
        tok = i * tt + t
        row_copy(tok, dest_ref[0, t]).start()
        row_copy(tok, dest_ref[1, t]).start()
        return carry
    lax.fori_loop(0, tt, issue, 0)

    def drain(t, carry):
        row_copy(0, 0).wait()
        row_copy(0, 0).wait()
        return carry
    lax.fori_loop(0, tt, drain, 0)


def _dispatch(meta, dest, h2rows, n_blocks, tt=512):
    n = dest.shape[1]
    p_rows = n_blocks * FFN_BLOCK * ROW_CHUNKS
    return pl.pallas_call(
        functools.partial(_dispatch_kernel, n_blocks=n_blocks),
        grid_spec=pltpu.PrefetchScalarGridSpec(
            num_scalar_prefetch=1,
            grid=(n // tt,),
            in_specs=[
                pl.BlockSpec((SUBLANES, tt), lambda i, m: (0, i), memory_space=pltpu.SMEM),
                pl.BlockSpec(memory_space=pl.ANY),
            ],
            out_specs=pl.BlockSpec(memory_space=pl.ANY),
            scratch_shapes=[
                pltpu.VMEM((FFN_BLOCK * ROW_CHUNKS, LANES), F32),
                pltpu.SemaphoreType.DMA,
                pltpu.SemaphoreType.DMA,
            ],
        ),
        out_shape=jax.ShapeDtypeStruct((p_rows, LANES), F32),
        compiler_params=_cparams(("arbitrary",)),
    )(meta, dest, h2rows)


def _ffn_kernel(meta_ref, x_ref, wg_ref, wu_ref, wd_ref, o_ref):
    i = pl.program_id(0)
    used = i < meta_ref[META_N_USED, 0]

    @pl.when(used)
    def _():
        x = jnp.concatenate([x_ref[pl.ds(c, FFN_BLOCK, stride=ROW_CHUNKS), :] for c in range(ROW_CHUNKS)],
                            axis=1).astype(BF16)
        g = jnp.dot(x, wg_ref[...], preferred_element_type=F32)
        u = jnp.dot(x, wu_ref[...], preferred_element_type=F32)
        hid = (g * jax.nn.sigmoid(g) * u).astype(BF16)
        y = jnp.dot(hid, wd_ref[...], preferred_element_type=F32)
        for c in range(ROW_CHUNKS):
            o_ref[pl.ds(c, FFN_BLOCK, stride=ROW_CHUNKS), :] = y[:, c * LANES:(c + 1) * LANES]

    @pl.when(jnp.logical_not(used))
    def _():
        o_ref[...] = jnp.zeros_like(o_ref)


def _ffn(meta, xs, wg_bf, wu_bf, wd_bf, n_blocks):
    blk_rows = FFN_BLOCK * ROW_CHUNKS
    row_spec = pl.BlockSpec((blk_rows, LANES), lambda i, m: (i, 0))
    return pl.pallas_call(
        _ffn_kernel,
        grid_spec=pltpu.PrefetchScalarGridSpec(
            num_scalar_prefetch=1,
            grid=(n_blocks,),
            in_specs=[
                row_spec,
                pl.BlockSpec((None, D_MODEL, D_EXPERT), lambda i, m: (m[META_BLOCK_E, i], 0, 0)),
                pl.BlockSpec((None, D_MODEL, D_EXPERT), lambda i, m: (m[META_BLOCK_E, i], 0, 0)),
                pl.BlockSpec((None, D_EXPERT, D_MODEL), lambda i, m: (m[META_BLOCK_E, i], 0, 0)),
            ],
            out_specs=row_spec,
        ),
        out_shape=jax.ShapeDtypeStruct(xs.shape, F32),
        compiler_params=_cparams(("arbitrary",)),
    )(meta, xs, wg_bf, wu_bf, wd_bf)


def _combine_kernel(dest_ref, wts_ref, x1_ref, g2_ref, fg_ref, yb_hbm, o_ref, buf_ref, sem):
    tt = x1_ref.shape[0]

    def row_copy(k, t, slot):
        src = pl.multiple_of(slot * ROW_CHUNKS, ROW_CHUNKS)
        dst = pl.multiple_of(t * ROW_CHUNKS, ROW_CHUNKS)
        return pltpu.make_async_copy(yb_hbm.at[pl.ds(src, ROW_CHUNKS)], buf_ref.at[k, pl.ds(dst, ROW_CHUNKS)], sem)

    def issue(t, carry):
        row_copy(0, t, dest_ref[0, t]).start()
        row_copy(1, t, dest_ref[1, t]).start()
        return carry
    lax.fori_loop(0, tt, issue, 0)

    def drain(t, carry):
        row_copy(0, 0, 0).wait()
        row_copy(1, 0, 0).wait()
        return carry
    lax.fori_loop(0, tt, drain, 0)

    eye = jnp.where(lax.broadcasted_iota(I32, (SUBLANES, LANES), 0) == lax.broadcasted_iota(I32, (SUBLANES, LANES), 1),
                    1.0, 0.0)
    wcol = lax.dot_general(wts_ref[...], eye, (((0,), (0,)), ((), ())), precision=HIGHEST,
                           preferred_element_type=F32)
    ya = jnp.concatenate([buf_ref[0, pl.ds(c, tt, stride=ROW_CHUNKS), :] for c in range(ROW_CHUNKS)], axis=1)
    yb = jnp.concatenate([buf_ref[1, pl.ds(c, tt, stride=ROW_CHUNKS), :] for c in range(ROW_CHUNKS)], axis=1)
    y = ya * wcol[:, 0:1] + yb * wcol[:, 1:2]
    x2 = x1_ref[...] + g2_ref[...] * y
    o_ref[...] = x2 * lax.rsqrt(jnp.mean(x2 * x2, axis=-1, keepdims=True) + RMS_EPS) * fg_ref[...]


def _combine(dest, wts, x1, mod4, final_g, yb, seq, tt=256):
    n = x1.shape[0]
    spt = seq // tt
    return pl.pallas_call(
        _combine_kernel,
        grid=(n // tt,),
        in_specs=[
            pl.BlockSpec((SUBLANES, tt), lambda i: (0, i), memory_space=pltpu.SMEM),
            pl.BlockSpec((SUBLANES, tt), lambda i: (0, i)),
            pl.BlockSpec((tt, D_MODEL), lambda i: (i, 0)),
            pl.BlockSpec((None, None, 1, D_MODEL), lambda i: (5, i // spt, 0, 0)),
            pl.BlockSpec((1, D_MODEL), lambda i: (0, 0)),
            pl.BlockSpec(memory_space=pl.ANY),
        ],
        out_specs=pl.BlockSpec((tt, D_MODEL), lambda i: (i, 0)),
        out_shape=jax.ShapeDtypeStruct((n, D_MODEL), F32),
        scratch_shapes=[pltpu.VMEM((2, tt * ROW_CHUNKS, LANES), F32), pltpu.SemaphoreType.DMA],
        compiler_params=_cparams(("arbitrary",)),
    )(dest, wts, x1, mod4, final_g, yb)


def kernel(x, c, w_ada, b_ada, norm1_g, w_in, lambda_q1, lambda_k1, lambda_q2, lambda_k2, subln_g, w_pool,
           pool_scale, w_out, norm2_g, w_router_group, b_router_group, w_router_expert, b_router_expert,
           w_gate, w_up, w_down, final_g):
    bsz, seq, d = x.shape
    n = bsz * seq
    x2 = x.reshape(n, d)

    mod = _ada(c, w_ada[0], b_ada[0])
    mod4 = mod.reshape(mod.shape[0], bsz, 1, d)

    q, k, v, u = _inproj(x2, mod4, norm1_g, w_in[0].astype(BF16), seq)
    a_out = _attention(q, k, v, lambda_q1, lambda_k1, lambda_q2, lambda_k2, subln_g, bsz, seq)

    pad_rows = SUBLANES - N_GROUPS
    wr = jnp.concatenate([w_router_expert[0].T, w_router_group[0].T, jnp.zeros((pad_rows, d), F32)], axis=0)
    br = jnp.concatenate([b_router_expert[0], b_router_group[0], jnp.zeros((pad_rows,), F32)])
    br = jnp.broadcast_to(br[:, None], (br.shape[0], LANES))
    x1, h2rows, eid, wts = _mix(a_out, u, x2, mod4, w_pool[0].astype(BF16),
                                pool_scale[0].reshape(len(POOL_WINDOWS), 1, POOL_GROUP_DIM),
                                w_out[0].astype(BF16), norm2_g, wr, br, bsz, seq)

    n_blocks = pl.cdiv(2 * n, FFN_BLOCK) + N_EXPERTS
    dest, meta = _plan(eid, n_blocks)
    xs = _dispatch(meta, dest, h2rows, n_blocks)
    yb = _ffn(meta, xs, w_gate[0].astype(BF16), w_up[0].astype(BF16), w_down[0].astype(BF16), n_blocks)
    out = _combine(dest, wts, x1, mod4, final_g.reshape(1, d), yb, seq)
    return out.reshape(bsz, seq, d)
```

```python
import functools
import math

import jax
import jax.numpy as jnp
from jax import lax
from jax.experimental import pallas as pl
from jax.experimental.pallas import tpu as pltpu

F32 = jnp.float32
BF16 = jnp.bfloat16
I32 = jnp.int32
HIGHEST = lax.Precision.HIGHEST

D_MODEL = 1024
ATTN_WIDTH = 512
N_HEADS = 4
HEAD_DIM = 64
VALUE_DIM = 128
ROT_DIM = 16
ROPE_THETA = 500000.0
POOL_WINDOWS = (2, 4, 8, 16)
POOL_GROUP_DIM = 128
N_GROUPS = 4
EXPERTS_PER_GROUP = 8
N_EXPERTS = 32
D_EXPERT = 512
RMS_EPS = 1e-6
LAMBDA_INIT = 0.8 - 0.6 * math.exp(-0.3 * 0)

LANES = 128
SUBLANES = 8
ROW_CHUNKS = D_MODEL // LANES

POOL_HALO = 8
FFN_BLOCK = 256
DMA_ISSUE_UNROLL = 8
DMA_DRAIN_UNROLL = 16
ROUTER_ROWS = 48
VMEM_LIMIT = 56 * 1024 * 1024


def _cparams(sem, vmem=VMEM_LIMIT):
    return pltpu.CompilerParams(dimension_semantics=sem, vmem_limit_bytes=vmem)


def _ada_kernel(c_ref, w_ref, b_ref, o_ref):
    c = c_ref[...]
    act = c * jax.nn.sigmoid(c)
    o_ref[...] = jnp.dot(act, w_ref[...], precision=HIGHEST, preferred_element_type=F32) + b_ref[...]


def _ada(c, w_ada, b_ada):
    bsz = c.shape[0]
    n_mod = w_ada.shape[1] // D_MODEL
    return pl.pallas_call(
        _ada_kernel,
        grid=(n_mod,),
        in_specs=[
            pl.BlockSpec((bsz, D_MODEL), lambda j: (0, 0)),
            pl.BlockSpec((D_MODEL, D_MODEL), lambda j: (0, j)),
            pl.BlockSpec((1, D_MODEL), lambda j: (0, j)),
        ],
        out_specs=pl.BlockSpec((None, bsz, D_MODEL), lambda j: (j, 0, 0)),
        out_shape=jax.ShapeDtypeStruct((n_mod, bsz, D_MODEL), F32),
        compiler_params=_cparams(("arbitrary",)),
    )(c, w_ada, b_ada.reshape(1, -1))


def _inproj_kernel(x_ref, sh_ref, sc_ref, g_ref, w_ref, cos_ref, sa_ref, sb_ref,
                   q_ref, k_ref, v_ref, u_ref):
    x = x_ref[...]
    y = x * lax.rsqrt(jnp.mean(x * x, axis=-1, keepdims=True) + RMS_EPS) * g_ref[...]
    h = (y * (1.0 + sc_ref[...]) + sh_ref[...]).astype(BF16)
    cos, sa, sb = cos_ref[...], sa_ref[...], sb_ref[...]
    for part, o_ref, scale in ((0, q_ref, HEAD_DIM ** -0.5 * math.log2(math.e)), (1, k_ref, None)):
        z = jnp.dot(h, w_ref[:, part * ATTN_WIDTH:(part + 1) * ATTN_WIDTH], preferred_element_type=F32)
        for hd in range(N_HEADS):
            zc = z[:, hd * LANES:(hd + 1) * LANES]
            r = zc * cos + pltpu.roll(zc, ROT_DIM // 2, 1) * sa + pltpu.roll(zc, LANES - ROT_DIM // 2, 1) * sb
            if scale is not None:
                r = r * scale
            o_ref[:, hd * LANES:(hd + 1) * LANES] = r.astype(BF16)
    v_ref[...] = jnp.dot(h, w_ref[:, 2 * ATTN_WIDTH:3 * ATTN_WIDTH], preferred_element_type=F32).astype(BF16)
    u_ref[...] = jnp.dot(h, w_ref[:, 3 * ATTN_WIDTH:], preferred_element_type=F32).astype(BF16)


def _rope_lane_tables(seq):
    pos = jnp.arange(seq, dtype=F32)
    inv_freq = ROPE_THETA ** (-jnp.arange(0, ROT_DIM, 2, dtype=F32) / ROT_DIM)
    ang = pos[:, None] * inv_freq[None, :]
    cos, sin = jnp.cos(ang), jnp.sin(ang)
    half = ROT_DIM // 2
    ones = jnp.ones((seq, HEAD_DIM - ROT_DIM), F32)
    zeros_h = jnp.zeros((seq, half), F32)
    zeros_r = jnp.zeros((seq, HEAD_DIM - ROT_DIM), F32)
    cos_t = jnp.concatenate([cos, cos, ones], axis=1)
    sa_t = jnp.concatenate([zeros_h, sin, zeros_r], axis=1)
    sb_t = jnp.concatenate([-sin, zeros_h, zeros_r], axis=1)
    rep = LANES // HEAD_DIM
    return jnp.tile(cos_t, (1, rep)), jnp.tile(sa_t, (1, rep)), jnp.tile(sb_t, (1, rep))


def _inproj(x2, mod4, norm1_g, w_in_bf, seq, tm=512):
    n = x2.shape[0]
    spt = seq // tm
    cos_t, sa_t, sb_t = _rope_lane_tables(seq)
    mod_spec = lambda which: pl.BlockSpec((None, None, 1, D_MODEL), lambda i: (which, i // spt, 0, 0))
    tab_spec = pl.BlockSpec((tm, LANES), lambda i: (i % spt, 0))
    out_spec = pl.BlockSpec((tm, ATTN_WIDTH), lambda i: (i, 0))
    out_sds = jax.ShapeDtypeStruct((n, ATTN_WIDTH), BF16)
    return pl.pallas_call(
        _inproj_kernel,
        grid=(n // tm,),
        in_specs=[
            pl.BlockSpec((tm, D_MODEL), lambda i: (i, 0)),
            mod_spec(0), mod_spec(1),
            pl.BlockSpec((1, D_MODEL), lambda i: (0, 0)),
            pl.BlockSpec((D_MODEL, 4 * ATTN_WIDTH), lambda i: (0, 0)),
            tab_spec, tab_spec, tab_spec,
        ],
        out_specs=[out_spec] * 4,
        out_shape=[out_sds] * 4,
        compiler_params=_cparams(("arbitrary",)),
    )(x2, mod4, mod4, norm1_g, w_in_bf, cos_t, sa_t, sb_t)


def _attn_kernel(lq1_ref, lk1_ref, lq2_ref, lk2_ref, sg_ref, q_ref, k_ref, v_ref, o_ref,
                 sa_ref, sb_ref, ma_ref, mb_ref, *, tq):
    lam = (jnp.exp(jnp.sum(lq1_ref[...] * lk1_ref[...], axis=-1, keepdims=True))
           - jnp.exp(jnp.sum(lq2_ref[...] * lk2_ref[...], axis=-1, keepdims=True)) + LAMBDA_INIT)
    n_tiles = q_ref.shape[0] // tq
    bufs = ((sa_ref, ma_ref), (sb_ref, mb_ref))

    def score_stage(j, s_ref, m_ref):
        q = q_ref[j * tq:(j + 1) * tq, :]
        lane = lax.broadcasted_iota(I32, q.shape, 1)
        zero = jnp.zeros_like(q)
        qq = jnp.concatenate([jnp.where(lane < HEAD_DIM, q, zero), jnp.where(lane >= HEAD_DIM, q, zero)], axis=0)
        s = lax.dot_general(qq, k_ref[...], (((1,), (1,)), ((), ())), preferred_element_type=F32)
        s_ref[...] = s
        m_ref[...] = jnp.max(s, axis=-1, keepdims=True)

    def value_stage(j, s_ref, m_ref):
        p = jnp.exp2(s_ref[...] - m_ref[...])
        inv = 1.0 / jnp.sum(p, axis=-1, keepdims=True)
        w = p[:tq] * inv[:tq] - p[tq:] * (lam * inv[tq:])
        o = jnp.dot(w.astype(BF16), v_ref[...], preferred_element_type=F32)
        o = o * lax.rsqrt(jnp.mean(o * o, axis=-1, keepdims=True) + RMS_EPS) * sg_ref[...]
        o_ref[j * tq:(j + 1) * tq, :] = (o * (1.0 - LAMBDA_INIT)).astype(BF16)

    score_stage(0, *bufs[0])
    for j in range(n_tiles):
        if j + 1 < n_tiles:
            score_stage(j + 1, *bufs[(j + 1) % 2])
        value_stage(j, *bufs[j % 2])


def _attention(q, k, v, lq1, lk1, lq2, lk2, subln_g, bsz, seq, tq=256):
    n = q.shape[0]
    vec_spec = lambda width: pl.BlockSpec((1, width), lambda b, h: (0, 0))
    head_spec = pl.BlockSpec((seq, LANES), lambda b, h: (b, h))
    return pl.pallas_call(
        functools.partial(_attn_kernel, tq=tq),
        grid=(bsz, N_HEADS),
        in_specs=[vec_spec(HEAD_DIM)] * 4 + [vec_spec(VALUE_DIM), head_spec, head_spec, head_spec],
        out_specs=head_spec,
        out_shape=jax.ShapeDtypeStruct((n, ATTN_WIDTH), BF16),
        scratch_shapes=[pltpu.VMEM((2 * tq, seq), F32), pltpu.VMEM((2 * tq, seq), F32),
                        pltpu.VMEM((2 * tq, 1), F32), pltpu.VMEM((2 * tq, 1), F32)],
        compiler_params=_cparams(("arbitrary",) * 2),
    )(lq1, lk1, lq2, lk2, subln_g, q, k, v)


def _mix_kernel(a_ref, u_ref, x_ref, g1_ref, sh2_ref, sc2_ref, wp_ref, ps_ref, wo_ref, n2_ref,
                wr_ref, br_ref, x1_ref, h2_ref, eid_ref, wts_ref, up_ref, *, seq):
    j = pl.program_id(1)
    tm = x_ref.shape[0]
    pool_w = u_ref.shape[1]

    @pl.when(j == 0)
    def _():
        up_ref[0:POOL_HALO, :] = jnp.zeros((POOL_HALO, pool_w), F32)
        up_ref[POOL_HALO + seq:, :] = jnp.zeros((POOL_HALO, pool_w), F32)
        up_ref[POOL_HALO:POOL_HALO + seq, :] = u_ref[...].astype(F32)

    r0 = pl.multiple_of(j * tm, tm)
    pos = r0 + lax.broadcasted_iota(I32, (tm, 1), 0)
    heads = [a_ref[...]]
    for g, win in enumerate(POOL_WINDOWS):
        left = win // 2
        right = win - 1 - left
        cols = slice(g * POOL_GROUP_DIM, (g + 1) * POOL_GROUP_DIM)
        win_rows = up_ref[pl.ds(r0, tm + 2 * POOL_HALO), cols]
        tot, span = win_rows, 1
        while span < win:
            tot = tot[:tot.shape[0] - span] + tot[span:]
            span *= 2
        tot = tot[POOL_HALO - left:POOL_HALO - left + tm]
        cnt = (jnp.minimum(pos + (right + 1), seq) - jnp.maximum(pos - left, 0)).astype(F32)
        d = tot / cnt - win_rows[POOL_HALO:POOL_HALO + tm]
        yg = jnp.dot(d.astype(BF16), wp_ref[g], preferred_element_type=F32) * ps_ref[g]
        heads.append(yg.astype(BF16))
    mix = jnp.dot(jnp.concatenate(heads, axis=1), wo_ref[...], preferred_element_type=F32)
    x1 = x_ref[...] + g1_ref[...] * mix
    x1_ref[...] = x1
    y = x1 * lax.rsqrt(jnp.mean(x1 * x1, axis=-1, keepdims=True) + RMS_EPS) * n2_ref[...]
    h2 = y * (1.0 + sc2_ref[...]) + sh2_ref[...]
    for c in range(ROW_CHUNKS):
        h2_ref[pl.ds(c, tm, stride=ROW_CHUNKS), :] = h2[:, c * LANES:(c + 1) * LANES]

    n_rt = br_ref.shape[0]
    nt_dims = (((1,), (1,)), ((), ()))
    h_hi = h2.astype(BF16)
    h_lo = (h2 - h_hi.astype(F32)).astype(BF16)
    lt2 = lax.dot_general(wr_ref[...], h_hi, nt_dims, preferred_element_type=F32)
    lt = (lt2[:n_rt] + lt2[n_rt:] + lax.dot_general(wr_ref[0:n_rt, :], h_lo, nt_dims, preferred_element_type=F32)
          + br_ref[:, 0:1])
    gl = [lt[N_EXPERTS + i:N_EXPERTS + i + 1, :] for i in range(N_GROUPS)]
    gmax = jnp.maximum(jnp.maximum(gl[0], gl[1]), jnp.maximum(gl[2], gl[3]))
    gsel = jnp.where(gl[0] == gmax, 0, jnp.where(gl[1] == gmax, 1, jnp.where(gl[2] == gmax, 2, 3)))
    gden = sum(jnp.exp(g_ - gmax) for g_ in gl)
    e_in = lt[3 * EXPERTS_PER_GROUP:4 * EXPERTS_PER_GROUP, :]
    for i in (2, 1, 0):
        e_in = jnp.where(gsel == i, lt[i * EXPERTS_PER_GROUP:(i + 1) * EXPERTS_PER_GROUP, :], e_in)
    sub = lax.broadcasted_iota(I32, e_in.shape, 0)
    m1 = jnp.max(e_in, axis=0, keepdims=True)
    i1 = jnp.min(jnp.where(e_in == m1, sub, EXPERTS_PER_GROUP), axis=0, keepdims=True)
    rest = jnp.where(sub == i1, -jnp.inf, e_in)
    m2 = jnp.max(rest, axis=0, keepdims=True)
    i2 = jnp.min(jnp.where(rest == m2, sub, EXPERTS_PER_GROUP), axis=0, keepdims=True)
    p2 = jnp.exp(m2 - m1)
    w1 = 1.0 / ((1.0 + p2) * gden)
    w2 = p2 * w1
    row = lax.broadcasted_iota(I32, (SUBLANES, tm), 0)
    base = gsel * EXPERTS_PER_GROUP
    eid_ref[...] = jnp.where(row == 0, base + i1, jnp.where(row == 1, base + i2, 0))
    wts_ref[...] = jnp.where(row == 0, w1, jnp.where(row == 1, w2, 0.0))


def _mix(a_out, u, x2, mod4, w_pool_bf, pool_scale, w_out_bf, norm2_g, wr, br, bsz, seq, tm=512):
    n = x2.shape[0]
    spt = seq // tm
    mod_spec = lambda which: pl.BlockSpec((None, None, 1, D_MODEL), lambda b, j: (which, b, 0, 0))
    tok_spec = lambda width: pl.BlockSpec((tm, width), lambda b, j: (b * spt + j, 0))
    lane_spec = pl.BlockSpec((SUBLANES, tm), lambda b, j: (0, b * spt + j))
    return pl.pallas_call(
        functools.partial(_mix_kernel, seq=seq),
        grid=(bsz, spt),
        in_specs=[
            tok_spec(ATTN_WIDTH),
            pl.BlockSpec((seq, ATTN_WIDTH), lambda b, j: (b, 0)),
            tok_spec(D_MODEL),
            mod_spec(2), mod_spec(3), mod_spec(4),
            pl.BlockSpec((len(POOL_WINDOWS), POOL_GROUP_DIM, POOL_GROUP_DIM), lambda b, j: (0, 0, 0)),
            pl.BlockSpec((len(POOL_WINDOWS), 1, POOL_GROUP_DIM), lambda b, j: (0, 0, 0)),
            pl.BlockSpec((D_MODEL, D_MODEL), lambda b, j: (0, 0)),
            pl.BlockSpec((1, D_MODEL), lambda b, j: (0, 0)),
            pl.BlockSpec((2 * ROUTER_ROWS, D_MODEL), lambda b, j: (0, 0)),
            pl.BlockSpec((ROUTER_ROWS, LANES), lambda b, j: (0, 0)),
        ],
        out_specs=[
            tok_spec(D_MODEL),
            pl.BlockSpec((tm * ROW_CHUNKS, LANES), lambda b, j: (b * spt + j, 0)),
            lane_spec, lane_spec,
        ],
        out_shape=[
            jax.ShapeDtypeStruct((n, D_MODEL), F32),
            jax.ShapeDtypeStruct((n * ROW_CHUNKS, LANES), F32),
            jax.ShapeDtypeStruct((SUBLANES, n), I32),
            jax.ShapeDtypeStruct((SUBLANES, n), F32),
        ],
        scratch_shapes=[pltpu.VMEM((seq + 2 * POOL_HALO, ATTN_WIDTH), F32)],
        compiler_params=_cparams(("arbitrary", "arbitrary")),
    )(a_out, u, x2, mod4, mod4, mod4, w_pool_bf, pool_scale, w_out_bf, norm2_g, wr, br)


META_BLOCK_E, META_N_USED, META_PAD_END, META_COUNT = 0, 1, 2, 3


def _plan_kernel(eid_ref, dest_ref, meta_ref, cnt_ref, base_ref, *, n_blocks, meta_w):
    phase = pl.program_id(0)
    t = pl.program_id(1)
    nt = pl.num_programs(1)
    tt = eid_ref.shape[1]
    sub = lax.broadcasted_iota(I32, (N_EXPERTS, tt), 0)
    oh0 = sub == eid_ref[0:1, :]
    oh1 = sub == eid_ref[1:2, :]
    oh = jnp.where(oh0 | oh1, 1.0, 0.0)

    @pl.when((phase == 0) & (t == 0))
    def _():
        cnt_ref[...] = jnp.zeros_like(cnt_ref)

    @pl.when(phase == 0)
    def _():
        cnt_ref[...] += jnp.sum(oh, axis=1, keepdims=True)

    @pl.when((phase == 0) & (t == nt - 1))
    def _():
        cnt = cnt_ref[...]
        padded = jnp.floor((cnt + (FFN_BLOCK - 1)) * (1.0 / FFN_BLOCK)) * FFN_BLOCK
        er = lax.broadcasted_iota(I32, (N_EXPERTS, N_EXPERTS), 0)
        ec = lax.broadcasted_iota(I32, (N_EXPERTS, N_EXPERTS), 1)
        tri = jnp.where(ec <= er, 1.0, 0.0)
        pad_end = jnp.dot(tri, padded, precision=HIGHEST, preferred_element_type=F32)
        base_ref[...] = pad_end - padded
        blk_row = (lax.broadcasted_iota(I32, (N_EXPERTS, meta_w), 1) * FFN_BLOCK).astype(F32)
        block_e = jnp.sum(jnp.where(pad_end[:, 0:1] <= blk_row, 1.0, 0.0), axis=0, keepdims=True)
        block_e = jnp.minimum(block_e, N_EXPERTS - 1.0)
        n_used = pad_end[N_EXPERTS - 1:N_EXPERTS, 0:1] * (1.0 / FFN_BLOCK)
        diag = (lax.broadcasted_iota(I32, (N_EXPERTS, LANES), 0)
                == lax.broadcasted_iota(I32, (N_EXPERTS, LANES), 1))
        pe_lane = jnp.sum(jnp.where(diag, pad_end, 0.0), axis=0, keepdims=True)
        cnt_lane = jnp.sum(jnp.where(diag, cnt, 0.0), axis=0, keepdims=True)
        zpad = jnp.zeros((1, meta_w - LANES), F32)
        row = lax.broadcasted_iota(I32, (SUBLANES, meta_w), 0)
        meta = jnp.where(row == META_BLOCK_E, block_e,
                         jnp.where(row == META_N_USED, n_used,
                                   jnp.where(row == META_PAD_END, jnp.concatenate([pe_lane, zpad], axis=1),
                                             jnp.where(row == META_COUNT, jnp.concatenate([cnt_lane, zpad], axis=1),
                                                       0.0))))
        meta_ref[...] = meta.astype(I32)

    @pl.when(phase == 1)
    def _():
        r = lax.broadcasted_iota(I32, (tt, tt), 0)
        c = lax.broadcasted_iota(I32, (tt, tt), 1)
        triu = jnp.where(r <= c, 1.0, 0.0).astype(BF16)
        incl = jnp.dot(oh.astype(BF16), triu, preferred_element_type=F32)
        slot = base_ref[:, 0:1] + incl - oh
        d0 = jnp.sum(jnp.where(oh0, slot, 0.0), axis=0, keepdims=True)
        d1 = jnp.sum(jnp.where(oh1, slot, 0.0), axis=0, keepdims=True)
        row = lax.broadcasted_iota(I32, (SUBLANES, tt), 0)
        dest_ref[...] = jnp.where(row == 0, d0, jnp.where(row == 1, d1, 0.0)).astype(I32)
        base_ref[...] += incl[:, tt - 1:tt]


def _plan(eid, n_blocks, tt=512):
    n = eid.shape[1]
    meta_w = pl.cdiv(n_blocks, LANES) * LANES
    return pl.pallas_call(
        functools.partial(_plan_kernel, n_blocks=n_blocks, meta_w=meta_w),
        grid=(2, n // tt),
        in_specs=[pl.BlockSpec((SUBLANES, tt), lambda p, t: (0, t))],
        out_specs=[
            pl.BlockSpec((SUBLANES, tt), lambda p, t: (0, p * t)),
            pl.BlockSpec((SUBLANES, meta_w), lambda p, t: (0, 0)),
        ],
        out_shape=[
            jax.ShapeDtypeStruct((SUBLANES, n), I32),
            jax.ShapeDtypeStruct((SUBLANES, meta_w), I32),
        ],
        scratch_shapes=[pltpu.VMEM((N_EXPERTS, LANES), F32), pltpu.VMEM((N_EXPERTS, LANES), F32)],
        compiler_params=_cparams(("arbitrary", "arbitrary")),
    )(eid)


def _dispatch_kernel(meta_ref, dest_ref, h2_ref, xs_hbm, zero_ref, zsem, sem, *, n_blocks):
    i = pl.program_id(0)
    tt = dest_ref.shape[1]
    blk_rows = FFN_BLOCK * ROW_CHUNKS

    def zero_copy(blk):
        start = pl.multiple_of(blk * blk_rows, blk_rows)
        return pltpu.make_async_copy(zero_ref, xs_hbm.at[pl.ds(start, blk_rows)], zsem)

    @pl.when(i == 0)
    def _():
        zero_ref[...] = jnp.zeros_like(zero_ref)
        for e in range(N_EXPERTS):
            @pl.when(meta_ref[META_COUNT, e] > 0)
            def _():
                blk = meta_ref[META_PAD_END, e] // FFN_BLOCK - 1
                cp = zero_copy(blk)
                cp.start()
                cp.wait()

        def tail(blk, carry):
            cp = zero_copy(blk)
            cp.start()
            cp.wait()
            return carry
        lax.fori_loop(meta_ref[META_N_USED, 0], n_blocks, tail, 0)

    def row_copy(t, slot):
        src = pl.multiple_of(t * ROW_CHUNKS, ROW_CHUNKS)
        dst = pl.multiple_of(slot * ROW_CHUNKS, ROW_CHUNKS)
        return pltpu.make_async_copy(h2_ref.at[pl.ds(src, ROW_CHUNKS)], xs_hbm.at[pl.ds(dst, ROW_CHUNKS)], sem)

    def issue(g, carry):
        for u in range(DMA_ISSUE_UNROLL):
            t = g * DMA_ISSUE_UNROLL + u
            row_copy(t, dest_ref[0, t]).start()
            row_copy(t, dest_ref[1, t]).start()
        return carry
    lax.fori_loop(0, tt // DMA_ISSUE_UNROLL, issue, 0)

    def drain(g, carry):
        for _ in range(DMA_DRAIN_UNROLL):
            row_copy(0, 0).wait()
        return carry
    lax.fori_loop(0, 2 * tt // DMA_DRAIN_UNROLL, drain, 0)


def _dispatch(meta, dest, h2rows, n_blocks, tt=512):
    n = dest.shape[1]
    p_rows = n_blocks * FFN_BLOCK * ROW_CHUNKS
    return pl.pallas_call(
        functools.partial(_dispatch_kernel, n_blocks=n_blocks),
        grid_spec=pltpu.PrefetchScalarGridSpec(
            num_scalar_prefetch=1,
            grid=(n // tt,),
            in_specs=[
                pl.BlockSpec((SUBLANES, tt), lambda i, m: (0, i), memory_space=pltpu.SMEM),
                pl.BlockSpec((tt * ROW_CHUNKS, LANES), lambda i, m: (i, 0)),
            ],
            out_specs=pl.BlockSpec(memory_space=pl.ANY),
            scratch_shapes=[
                pltpu.VMEM((FFN_BLOCK * ROW_CHUNKS, LANES), F32),
                pltpu.SemaphoreType.DMA,
                pltpu.SemaphoreType.DMA,
            ],
        ),
        out_shape=jax.ShapeDtypeStruct((p_rows, LANES), F32),
        compiler_params=_cparams(("arbitrary",)),
    )(meta, dest, h2rows)


def _ffn_kernel(meta_ref, x_ref, wg_ref, wu_ref, wd_ref, o_ref, wgu_bf_ref, wd_bf_ref):
    i = pl.program_id(0)
    used = i < meta_ref[META_N_USED, 0]
    expert = meta_ref[META_BLOCK_E, i]
    prev_expert = meta_ref[META_BLOCK_E, jnp.maximum(i - 1, 0)]

    @pl.when((i == 0) | (expert != prev_expert))
    def _():
        wgu_bf_ref[:, 0:D_EXPERT] = wg_ref[...].astype(BF16)
        wgu_bf_ref[:, D_EXPERT:] = wu_ref[...].astype(BF16)
        wd_bf_ref[...] = wd_ref[...].astype(BF16)

    @pl.when(used)
    def _():
        x = jnp.concatenate([x_ref[pl.ds(c, FFN_BLOCK, stride=ROW_CHUNKS), :] for c in range(ROW_CHUNKS)],
                            axis=1).astype(BF16)
        gu = jnp.dot(x, wgu_bf_ref[...], preferred_element_type=F32)
        g, u = gu[:, 0:D_EXPERT], gu[:, D_EXPERT:]
        hid = (g * jax.nn.sigmoid(g) * u).astype(BF16)
        y = jnp.dot(hid, wd_bf_ref[...], preferred_element_type=F32)
        for c in range(ROW_CHUNKS):
            o_ref[pl.ds(c, FFN_BLOCK, stride=ROW_CHUNKS), :] = y[:, c * LANES:(c + 1) * LANES]

    @pl.when(jnp.logical_not(used))
    def _():
        o_ref[...] = jnp.zeros_like(o_ref)


def _ffn(meta, xs, w_gate, w_up, w_down, n_blocks):
    blk_rows = FFN_BLOCK * ROW_CHUNKS
    row_spec = pl.BlockSpec((blk_rows, LANES), lambda i, m: (i, 0))
    return pl.pallas_call(
        _ffn_kernel,
        grid_spec=pltpu.PrefetchScalarGridSpec(
            num_scalar_prefetch=1,
            grid=(n_blocks,),
            in_specs=[
                row_spec,
                pl.BlockSpec((None, D_MODEL, D_EXPERT), lambda i, m: (m[META_BLOCK_E, i], 0, 0)),
                pl.BlockSpec((None, D_MODEL, D_EXPERT), lambda i, m: (m[META_BLOCK_E, i], 0, 0)),
                pl.BlockSpec((None, D_EXPERT, D_MODEL), lambda i, m: (m[META_BLOCK_E, i], 0, 0)),
            ],
            out_specs=row_spec,
            scratch_shapes=[pltpu.VMEM((D_MODEL, 2 * D_EXPERT), BF16), pltpu.VMEM((D_EXPERT, D_MODEL), BF16)],
        ),
        out_shape=jax.ShapeDtypeStruct(xs.shape, F32),
        compiler_params=_cparams(("arbitrary",)),
    )(meta, xs, w_gate, w_up, w_down)


def _combine_kernel(dest_ref, next_dest_ref, wts_ref, x1_ref, g2_ref, fg_ref, yb_hbm, o_ref, buf_ref, sems):
    i = pl.program_id(0)
    last = pl.num_programs(0) - 1
    tt = x1_ref.shape[0]
    slot = i % 2

    def row_copy(buf_slot, k, t, row):
        src = pl.multiple_of(row * ROW_CHUNKS, ROW_CHUNKS)
        dst = pl.multiple_of(t * ROW_CHUNKS, ROW_CHUNKS)
        return pltpu.make_async_copy(yb_hbm.at[pl.ds(src, ROW_CHUNKS)],
                                     buf_ref.at[buf_slot, k, pl.ds(dst, ROW_CHUNKS)], sems.at[buf_slot])

    def issue_tile(idx_ref, buf_slot):
        def body(g, carry):
            for u in range(DMA_ISSUE_UNROLL):
                t = g * DMA_ISSUE_UNROLL + u
                row_copy(buf_slot, 0, t, idx_ref[0, t]).start()
                row_copy(buf_slot, 1, t, idx_ref[1, t]).start()
            return carry
        lax.fori_loop(0, tt // DMA_ISSUE_UNROLL, body, 0)

    @pl.when(i == 0)
    def _():
        issue_tile(dest_ref, 0)

    @pl.when(i < last)
    def _():
        issue_tile(next_dest_ref, 1 - slot)

    def drain(g, carry):
        for _ in range(DMA_DRAIN_UNROLL):
            row_copy(slot, 0, 0, 0).wait()
        return carry
    lax.fori_loop(0, 2 * tt // DMA_DRAIN_UNROLL, drain, 0)

    eye = jnp.where(lax.broadcasted_iota(I32, (SUBLANES, LANES), 0) == lax.broadcasted_iota(I32, (SUBLANES, LANES), 1),
                    1.0, 0.0)
    wcol = lax.dot_general(wts_ref[...], eye, (((0,), (0,)), ((), ())), precision=HIGHEST,
                           preferred_element_type=F32)
    ya = jnp.concatenate([buf_ref[slot, 0, pl.ds(c, tt, stride=ROW_CHUNKS), :] for c in range(ROW_CHUNKS)], axis=1)
    yb = jnp.concatenate([buf_ref[slot, 1, pl.ds(c, tt, stride=ROW_CHUNKS), :] for c in range(ROW_CHUNKS)], axis=1)
    y = ya * wcol[:, 0:1] + yb * wcol[:, 1:2]
    x2 = x1_ref[...] + g2_ref[...] * y
    o_ref[...] = x2 * lax.rsqrt(jnp.mean(x2 * x2, axis=-1, keepdims=True) + RMS_EPS) * fg_ref[...]


def _combine(dest, wts, x1, mod4, final_g, yb, seq, tt=256):
    n = x1.shape[0]
    spt = seq // tt
    steps = n // tt
    return pl.pallas_call(
        _combine_kernel,
        grid=(steps,),
        in_specs=[
            pl.BlockSpec((SUBLANES, tt), lambda i: (0, i), memory_space=pltpu.SMEM),
            pl.BlockSpec((SUBLANES, tt), lambda i: (0, jnp.minimum(i + 1, steps - 1)), memory_space=pltpu.SMEM),
            pl.BlockSpec((SUBLANES, tt), lambda i: (0, i)),
            pl.BlockSpec((tt, D_MODEL), lambda i: (i, 0)),
            pl.BlockSpec((None, None, 1, D_MODEL), lambda i: (5, i // spt, 0, 0)),
            pl.BlockSpec((1, D_MODEL), lambda i: (0, 0)),
            pl.BlockSpec(memory_space=pl.ANY),
        ],
        out_specs=pl.BlockSpec((tt, D_MODEL), lambda i: (i, 0)),
        out_shape=jax.ShapeDtypeStruct((n, D_MODEL), F32),
        scratch_shapes=[pltpu.VMEM((2, 2, tt * ROW_CHUNKS, LANES), F32), pltpu.SemaphoreType.DMA((2,))],
        compiler_params=_cparams(("arbitrary",)),
    )(dest, dest, wts, x1, mod4, final_g, yb)


def kernel(x, c, w_ada, b_ada, norm1_g, w_in, lambda_q1, lambda_k1, lambda_q2, lambda_k2, subln_g, w_pool,
           pool_scale, w_out, norm2_g, w_router_group, b_router_group, w_router_expert, b_router_expert,
           w_gate, w_up, w_down, final_g):
    bsz, seq, d = x.shape
    n = bsz * seq
    x2 = x.reshape(n, d)

    mod = _ada(c, w_ada[0], b_ada[0])
    mod4 = mod.reshape(mod.shape[0], bsz, 1, d)

    q, k, v, u = _inproj(x2, mod4, norm1_g, w_in[0].astype(BF16), seq)
    a_out = _attention(q, k, v, lambda_q1, lambda_k1, lambda_q2, lambda_k2, subln_g, bsz, seq)

    pad_rows = ROUTER_ROWS - N_EXPERTS - N_GROUPS
    wr = jnp.concatenate([w_router_expert[0].T, w_router_group[0].T, jnp.zeros((pad_rows, d), F32)], axis=0)
    wr_hi = wr.astype(BF16)
    wr_lo = (wr - wr_hi.astype(F32)).astype(BF16)
    wr = jnp.concatenate([wr_hi, wr_lo], axis=0)
    br = jnp.concatenate([b_router_expert[0], b_router_group[0], jnp.zeros((pad_rows,), F32)])
    br = jnp.broadcast_to(br[:, None], (br.shape[0], LANES))
    x1, h2rows, eid, wts = _mix(a_out, u, x2, mod4, w_pool[0].astype(BF16),
                                pool_scale[0].reshape(len(POOL_WINDOWS), 1, POOL_GROUP_DIM),
                                w_out[0].astype(BF16), norm2_g, wr, br, bsz, seq)

    n_blocks = pl.cdiv(2 * n, FFN_BLOCK) + N_EXPERTS
    dest, meta = _plan(eid, n_blocks)
    xs = _dispatch(meta, dest, h2rows, n_blocks)
    yb = _ffn(meta, xs, w_gate[0], w_up[0], w_down[0], n_blocks)
    out = _combine(dest, wts, x1, mod4, final_g.reshape(1, d), yb, seq)
    return out.reshape(bsz, seq, d)
```

```python
import functools
import math

import jax
import jax.numpy as jnp
from jax import lax
from jax.experimental import pallas as pl
from jax.experimental.pallas import tpu as pltpu

F32 = jnp.float32
BF16 = jnp.bfloat16
I32 = jnp.int32
HIGHEST = lax.Precision.HIGHEST

D_MODEL = 1024
ATTN_WIDTH = 512
N_HEADS = 4
HEAD_DIM = 64
VALUE_DIM = 128
ROT_DIM = 16
ROPE_THETA = 500000.0
POOL_WINDOWS = (2, 4, 8, 16)
POOL_GROUP_DIM = 128
N_GROUPS = 4
EXPERTS_PER_GROUP = 8
N_EXPERTS = 32
TOP_K = 2
D_EXPERT = 512
RMS_EPS = 1e-6
LAMBDA_INIT = 0.8 - 0.6 * math.exp(-0.3 * 0)

LANES = 128
SUBLANES = 8
ROW_CHUNKS = D_MODEL // LANES

POOL_HALO = 8
FFN_BLOCK = 256
DMA_ISSUE_UNROLL = 8
DMA_DRAIN_UNROLL = 16
ROUTER_ROWS = 48
VMEM_LIMIT = 56 * 1024 * 1024


def _cparams(sem, vmem=VMEM_LIMIT):
    return pltpu.CompilerParams(dimension_semantics=sem, vmem_limit_bytes=vmem)


def _ada_kernel(c_ref, w_ref, b_ref, o_ref):
    c = c_ref[...]
    act = c * jax.nn.sigmoid(c)
    o_ref[...] = jnp.dot(act, w_ref[...], precision=HIGHEST, preferred_element_type=F32) + b_ref[...]


def _ada(c, w_ada, b_ada):
    bsz = c.shape[0]
    n_mod = w_ada.shape[1] // D_MODEL
    return pl.pallas_call(
        _ada_kernel,
        grid=(n_mod,),
        in_specs=[
            pl.BlockSpec((bsz, D_MODEL), lambda j: (0, 0)),
            pl.BlockSpec((D_MODEL, D_MODEL), lambda j: (0, j)),
            pl.BlockSpec((1, D_MODEL), lambda j: (0, j)),
        ],
        out_specs=pl.BlockSpec((None, bsz, D_MODEL), lambda j: (j, 0, 0)),
        out_shape=jax.ShapeDtypeStruct((n_mod, bsz, D_MODEL), F32),
        compiler_params=_cparams(("arbitrary",)),
    )(c, w_ada, b_ada.reshape(1, -1))


def _inproj_kernel(x_ref, sh_ref, sc_ref, g_ref, w_ref, cos_ref, sa_ref, sb_ref,
                   q_ref, k_ref, v_ref, u_ref):
    x = x_ref[...]
    y = x * lax.rsqrt(jnp.mean(x * x, axis=-1, keepdims=True) + RMS_EPS) * g_ref[...]
    h = (y * (1.0 + sc_ref[...]) + sh_ref[...]).astype(BF16)
    cos, sa, sb = cos_ref[...], sa_ref[...], sb_ref[...]
    for part, o_ref, scale in ((0, q_ref, HEAD_DIM ** -0.5 * math.log2(math.e)), (1, k_ref, None)):
        z = jnp.dot(h, w_ref[:, part * ATTN_WIDTH:(part + 1) * ATTN_WIDTH], preferred_element_type=F32)
        for hd in range(N_HEADS):
            zc = z[:, hd * LANES:(hd + 1) * LANES]
            r = zc * cos + pltpu.roll(zc, ROT_DIM // 2, 1) * sa + pltpu.roll(zc, LANES - ROT_DIM // 2, 1) * sb
            if scale is not None:
                r = r * scale
            o_ref[:, hd * LANES:(hd + 1) * LANES] = r.astype(BF16)
    v_ref[...] = jnp.dot(h, w_ref[:, 2 * ATTN_WIDTH:3 * ATTN_WIDTH], preferred_element_type=F32).astype(BF16)
    u_ref[...] = jnp.dot(h, w_ref[:, 3 * ATTN_WIDTH:], preferred_element_type=F32).astype(BF16)


def _rope_lane_tables(seq):
    pos = jnp.arange(seq, dtype=F32)
    inv_freq = ROPE_THETA ** (-jnp.arange(0, ROT_DIM, 2, dtype=F32) / ROT_DIM)
    ang = pos[:, None] * inv_freq[None, :]
    cos, sin = jnp.cos(ang), jnp.sin(ang)
    half = ROT_DIM // 2
    ones = jnp.ones((seq, HEAD_DIM - ROT_DIM), F32)
    zeros_h = jnp.zeros((seq, half), F32)
    zeros_r = jnp.zeros((seq, HEAD_DIM - ROT_DIM), F32)
    cos_t = jnp.concatenate([cos, cos, ones], axis=1)
    sa_t = jnp.concatenate([zeros_h, sin, zeros_r], axis=1)
    sb_t = jnp.concatenate([-sin, zeros_h, zeros_r], axis=1)
    rep = LANES // HEAD_DIM
    return jnp.tile(cos_t, (1, rep)), jnp.tile(sa_t, (1, rep)), jnp.tile(sb_t, (1, rep))


def _inproj(x2, mod4, norm1_g, w_in_bf, seq, tm=512):
    n = x2.shape[0]
    spt = seq // tm
    cos_t, sa_t, sb_t = _rope_lane_tables(seq)
    mod_spec = lambda which: pl.BlockSpec((None, None, 1, D_MODEL), lambda i: (which, i // spt, 0, 0))
    tab_spec = pl.BlockSpec((tm, LANES), lambda i: (i % spt, 0))
    out_spec = pl.BlockSpec((tm, ATTN_WIDTH), lambda i: (i, 0))
    out_sds = jax.ShapeDtypeStruct((n, ATTN_WIDTH), BF16)
    return pl.pallas_call(
        _inproj_kernel,
        grid=(n // tm,),
        in_specs=[
            pl.BlockSpec((tm, D_MODEL), lambda i: (i, 0)),
            mod_spec(0), mod_spec(1),
            pl.BlockSpec((1, D_MODEL), lambda i: (0, 0)),
            pl.BlockSpec((D_MODEL, 4 * ATTN_WIDTH), lambda i: (0, 0)),
            tab_spec, tab_spec, tab_spec,
        ],
        out_specs=[out_spec] * 4,
        out_shape=[out_sds] * 4,
        compiler_params=_cparams(("arbitrary",)),
    )(x2, mod4, mod4, norm1_g, w_in_bf, cos_t, sa_t, sb_t)


def _attn_kernel(lq1_ref, lk1_ref, lq2_ref, lk2_ref, sg_ref, q_ref, k_ref, v_ref, o_ref,
                 sa_ref, sb_ref, ma_ref, mb_ref, *, tq):
    lam = (jnp.exp(jnp.sum(lq1_ref[...] * lk1_ref[...], axis=-1, keepdims=True))
           - jnp.exp(jnp.sum(lq2_ref[...] * lk2_ref[...], axis=-1, keepdims=True)) + LAMBDA_INIT)
    n_tiles = q_ref.shape[0] // tq
    bufs = ((sa_ref, ma_ref), (sb_ref, mb_ref))

    def score_stage(j, s_ref, m_ref):
        q = q_ref[j * tq:(j + 1) * tq, :]
        lane = lax.broadcasted_iota(I32, q.shape, 1)
        zero = jnp.zeros_like(q)
        qq = jnp.concatenate([jnp.where(lane < HEAD_DIM, q, zero), jnp.where(lane >= HEAD_DIM, q, zero)], axis=0)
        s = lax.dot_general(qq, k_ref[...], (((1,), (1,)), ((), ())), preferred_element_type=F32)
        s_ref[...] = s
        m_ref[...] = jnp.max(s, axis=-1, keepdims=True)

    def value_stage(j, s_ref, m_ref):
        p = jnp.exp2(s_ref[...] - m_ref[...])
        l = jnp.sum(p, axis=-1, keepdims=True)
        pb = p.astype(BF16)
        ratio = (lam * l[:tq] / l[tq:]).astype(BF16)
        o = jnp.dot(pb[:tq] - pb[tq:] * ratio, v_ref[...], preferred_element_type=F32) / l[:tq]
        o = o * lax.rsqrt(jnp.mean(o * o, axis=-1, keepdims=True) + RMS_EPS) * sg_ref[...]
        o_ref[j * tq:(j + 1) * tq, :] = (o * (1.0 - LAMBDA_INIT)).astype(BF16)

    score_stage(0, *bufs[0])
    for j in range(n_tiles):
        if j + 1 < n_tiles:
            score_stage(j + 1, *bufs[(j + 1) % 2])
        value_stage(j, *bufs[j % 2])


def _attention(q, k, v, lq1, lk1, lq2, lk2, subln_g, bsz, seq, tq=256):
    n = q.shape[0]
    vec_spec = lambda width: pl.BlockSpec((1, width), lambda b, h: (0, 0))
    head_spec = pl.BlockSpec((seq, LANES), lambda b, h: (b, h))
    return pl.pallas_call(
        functools.partial(_attn_kernel, tq=tq),
        grid=(bsz, N_HEADS),
        in_specs=[vec_spec(HEAD_DIM)] * 4 + [vec_spec(VALUE_DIM), head_spec, head_spec, head_spec],
        out_specs=head_spec,
        out_shape=jax.ShapeDtypeStruct((n, ATTN_WIDTH), BF16),
        scratch_shapes=[pltpu.VMEM((2 * tq, seq), F32), pltpu.VMEM((2 * tq, seq), F32),
                        pltpu.VMEM((2 * tq, 1), F32), pltpu.VMEM((2 * tq, 1), F32)],
        compiler_params=_cparams(("arbitrary",) * 2),
    )(lq1, lk1, lq2, lk2, subln_g, q, k, v)


def _mix_kernel(a_ref, u_ref, x_ref, g1_ref, sh2_ref, sc2_ref, wp_ref, ps_ref, wo_ref, n2_ref,
                wr_ref, br_ref, x1_ref, h2_ref, eid_ref, wts_ref, up_ref, *, seq):
    j = pl.program_id(1)
    tm = x_ref.shape[0]
    pool_w = u_ref.shape[1]

    @pl.when(j == 0)
    def _():
        up_ref[0:POOL_HALO, :] = jnp.zeros((POOL_HALO, pool_w), F32)
        up_ref[POOL_HALO + seq:, :] = jnp.zeros((POOL_HALO, pool_w), F32)
        up_ref[POOL_HALO:POOL_HALO + seq, :] = u_ref[...].astype(F32)

    r0 = pl.multiple_of(j * tm, tm)
    pos = r0 + lax.broadcasted_iota(I32, (tm, 1), 0)
    heads = [a_ref[...]]
    for g, win in enumerate(POOL_WINDOWS):
        left = win // 2
        right = win - 1 - left
        cols = slice(g * POOL_GROUP_DIM, (g + 1) * POOL_GROUP_DIM)
        win_rows = up_ref[pl.ds(r0, tm + 2 * POOL_HALO), cols]
        tot, span = win_rows, 1
        while span < win:
            tot = tot[:tot.shape[0] - span] + tot[span:]
            span *= 2
        tot = tot[POOL_HALO - left:POOL_HALO - left + tm]
        cnt = (jnp.minimum(pos + (right + 1), seq) - jnp.maximum(pos - left, 0)).astype(F32)
        d = tot / cnt - win_rows[POOL_HALO:POOL_HALO + tm]
        yg = jnp.dot(d.astype(BF16), wp_ref[g], preferred_element_type=F32) * ps_ref[g]
        heads.append(yg.astype(BF16))
    mix = jnp.dot(jnp.concatenate(heads, axis=1), wo_ref[...], preferred_element_type=F32)
    x1 = x_ref[...] + g1_ref[...] * mix
    x1_ref[...] = x1
    y = x1 * lax.rsqrt(jnp.mean(x1 * x1, axis=-1, keepdims=True) + RMS_EPS) * n2_ref[...]
    h2 = y * (1.0 + sc2_ref[...]) + sh2_ref[...]
    for c in range(ROW_CHUNKS):
        h2_ref[pl.ds(c, tm, stride=ROW_CHUNKS), :] = h2[:, c * LANES:(c + 1) * LANES]

    n_rt = br_ref.shape[0]
    nt_dims = (((1,), (1,)), ((), ()))
    h_hi = h2.astype(BF16)
    h_lo = (h2 - h_hi.astype(F32)).astype(BF16)
    lt2 = lax.dot_general(wr_ref[...], h_hi, nt_dims, preferred_element_type=F32)
    lt = (lt2[:n_rt] + lt2[n_rt:] + lax.dot_general(wr_ref[0:n_rt, :], h_lo, nt_dims, preferred_element_type=F32)
          + br_ref[:, 0:1])
    gl = [lt[N_EXPERTS + i:N_EXPERTS + i + 1, :] for i in range(N_GROUPS)]
    gmax = jnp.maximum(jnp.maximum(gl[0], gl[1]), jnp.maximum(gl[2], gl[3]))
    gsel = jnp.where(gl[0] == gmax, 0, jnp.where(gl[1] == gmax, 1, jnp.where(gl[2] == gmax, 2, 3)))
    gden = sum(jnp.exp(g_ - gmax) for g_ in gl)
    e_in = lt[3 * EXPERTS_PER_GROUP:4 * EXPERTS_PER_GROUP, :]
    for i in (2, 1, 0):
        e_in = jnp.where(gsel == i, lt[i * EXPERTS_PER_GROUP:(i + 1) * EXPERTS_PER_GROUP, :], e_in)
    sub = lax.broadcasted_iota(I32, e_in.shape, 0)
    m1 = jnp.max(e_in, axis=0, keepdims=True)
    i1 = jnp.min(jnp.where(e_in == m1, sub, EXPERTS_PER_GROUP), axis=0, keepdims=True)
    rest = jnp.where(sub == i1, -jnp.inf, e_in)
    m2 = jnp.max(rest, axis=0, keepdims=True)
    i2 = jnp.min(jnp.where(rest == m2, sub, EXPERTS_PER_GROUP), axis=0, keepdims=True)
    p2 = jnp.exp(m2 - m1)
    w1 = 1.0 / ((1.0 + p2) * gden)
    w2 = p2 * w1
    row = lax.broadcasted_iota(I32, (SUBLANES, tm), 0)
    base = gsel * EXPERTS_PER_GROUP
    eid_ref[...] = jnp.where(row == 0, base + i1, jnp.where(row == 1, base + i2, 0))
    wts_ref[...] = jnp.where(row == 0, w1, jnp.where(row == 1, w2, 0.0))


def _mix(a_out, u, x2, mod4, w_pool_bf, pool_scale, w_out_bf, norm2_g, wr, br, bsz, seq, tm=512):
    n = x2.shape[0]
    spt = seq // tm
    mod_spec = lambda which: pl.BlockSpec((None, None, 1, D_MODEL), lambda b, j: (which, b, 0, 0))
    tok_spec = lambda width: pl.BlockSpec((tm, width), lambda b, j: (b * spt + j, 0))
    lane_spec = pl.BlockSpec((SUBLANES, tm), lambda b, j: (0, b * spt + j))
    return pl.pallas_call(
        functools.partial(_mix_kernel, seq=seq),
        grid=(bsz, spt),
        in_specs=[
            tok_spec(ATTN_WIDTH),
            pl.BlockSpec((seq, ATTN_WIDTH), lambda b, j: (b, 0)),
            tok_spec(D_MODEL),
            mod_spec(2), mod_spec(3), mod_spec(4),
            pl.BlockSpec((len(POOL_WINDOWS), POOL_GROUP_DIM, POOL_GROUP_DIM), lambda b, j: (0, 0, 0)),
            pl.BlockSpec((len(POOL_WINDOWS), 1, POOL_GROUP_DIM), lambda b, j: (0, 0, 0)),
            pl.BlockSpec((D_MODEL, D_MODEL), lambda b, j: (0, 0)),
            pl.BlockSpec((1, D_MODEL), lambda b, j: (0, 0)),
            pl.BlockSpec((2 * ROUTER_ROWS, D_MODEL), lambda b, j: (0, 0)),
            pl.BlockSpec((ROUTER_ROWS, LANES), lambda b, j: (0, 0)),
        ],
        out_specs=[
            tok_spec(D_MODEL),
            pl.BlockSpec((tm * ROW_CHUNKS, LANES), lambda b, j: (b * spt + j, 0)),
            lane_spec, lane_spec,
        ],
        out_shape=[
            jax.ShapeDtypeStruct((n, D_MODEL), F32),
            jax.ShapeDtypeStruct((n * ROW_CHUNKS, LANES), F32),
            jax.ShapeDtypeStruct((SUBLANES, n), I32),
            jax.ShapeDtypeStruct((SUBLANES, n), F32),
        ],
        scratch_shapes=[pltpu.VMEM((seq + 2 * POOL_HALO, ATTN_WIDTH), F32)],
        compiler_params=_cparams(("arbitrary", "arbitrary")),
    )(a_out, u, x2, mod4, mod4, mod4, w_pool_bf, pool_scale, w_out_bf, norm2_g, wr, br)


META_BLOCK_E, META_N_USED, META_PAD_END, META_COUNT = 0, 1, 2, 3


def _plan_kernel(eid_ref, dest_ref, meta_ref, cnt_ref, base_ref, *, n_blocks, meta_w, chunk):
    phase = pl.program_id(0)
    t = pl.program_id(1)
    nt = pl.num_programs(1)
    n_chunks = eid_ref.shape[1] // chunk
    sub = lax.broadcasted_iota(I32, (N_EXPERTS, chunk), 0)

    def one_hots(c):
        oh0 = sub == eid_ref[0:1, c * chunk:(c + 1) * chunk]
        oh1 = sub == eid_ref[1:2, c * chunk:(c + 1) * chunk]
        return oh0, oh1, jnp.where(oh0 | oh1, 1.0, 0.0)

    @pl.when((phase == 0) & (t == 0))
    def _():
        cnt_ref[...] = jnp.zeros_like(cnt_ref)

    @pl.when(phase == 0)
    def _():
        cnt_ref[...] += sum(jnp.sum(one_hots(c)[2], axis=1, keepdims=True) for c in range(n_chunks))

    @pl.when((phase == 0) & (t == nt - 1))
    def _():
        cnt = cnt_ref[...]
        padded = jnp.floor((cnt + (FFN_BLOCK - 1)) * (1.0 / FFN_BLOCK)) * FFN_BLOCK
        er = lax.broadcasted_iota(I32, (N_EXPERTS, N_EXPERTS), 0)
        ec = lax.broadcasted_iota(I32, (N_EXPERTS, N_EXPERTS), 1)
        tri = jnp.where(ec <= er, 1.0, 0.0)
        pad_end = jnp.dot(tri, padded, precision=HIGHEST, preferred_element_type=F32)
        base_ref[...] = pad_end - padded
        blk_row = (lax.broadcasted_iota(I32, (N_EXPERTS, meta_w), 1) * FFN_BLOCK).astype(F32)
        block_e = jnp.sum(jnp.where(pad_end[:, 0:1] <= blk_row, 1.0, 0.0), axis=0, keepdims=True)
        block_e = jnp.minimum(block_e, N_EXPERTS - 1.0)
        n_used = pad_end[N_EXPERTS - 1:N_EXPERTS, 0:1] * (1.0 / FFN_BLOCK)
        diag = (lax.broadcasted_iota(I32, (N_EXPERTS, LANES), 0)
                == lax.broadcasted_iota(I32, (N_EXPERTS, LANES), 1))
        pe_lane = jnp.sum(jnp.where(diag, pad_end, 0.0), axis=0, keepdims=True)
        cnt_lane = jnp.sum(jnp.where(diag, cnt, 0.0), axis=0, keepdims=True)
        zpad = jnp.zeros((1, meta_w - LANES), F32)
        row = lax.broadcasted_iota(I32, (SUBLANES, meta_w), 0)
        meta = jnp.where(row == META_BLOCK_E, block_e,
                         jnp.where(row == META_N_USED, n_used,
                                   jnp.where(row == META_PAD_END, jnp.concatenate([pe_lane, zpad], axis=1),
                                             jnp.where(row == META_COUNT, jnp.concatenate([cnt_lane, zpad], axis=1),
                                                       0.0))))
        meta_ref[...] = meta.astype(I32)

    @pl.when(phase == 1)
    def _():
        r = lax.broadcasted_iota(I32, (chunk, chunk), 0)
        c = lax.broadcasted_iota(I32, (chunk, chunk), 1)
        triu = jnp.where(r <= c, 1.0, 0.0).astype(BF16)
        row = lax.broadcasted_iota(I32, (SUBLANES, chunk), 0)
        base = base_ref[:, 0:1]
        for ci in range(n_chunks):
            oh0, oh1, oh = one_hots(ci)
            incl = jnp.dot(oh.astype(BF16), triu, preferred_element_type=F32)
            slot = base + incl - oh
            d0 = jnp.sum(jnp.where(oh0, slot, 0.0), axis=0, keepdims=True)
            d1 = jnp.sum(jnp.where(oh1, slot, 0.0), axis=0, keepdims=True)
            dest_ref[:, ci * chunk:(ci + 1) * chunk] = jnp.where(row == 0, d0, jnp.where(row == 1, d1, 0.0)).astype(I32)
            base = base + incl[:, chunk - 1:chunk]
        base_ref[...] = jnp.broadcast_to(base, base_ref.shape)


def _plan(eid, n_blocks, tt=2048, chunk=512):
    n = eid.shape[1]
    meta_w = pl.cdiv(n_blocks, LANES) * LANES
    return pl.pallas_call(
        functools.partial(_plan_kernel, n_blocks=n_blocks, meta_w=meta_w, chunk=chunk),
        grid=(2, n // tt),
        in_specs=[pl.BlockSpec((SUBLANES, tt), lambda p, t: (0, t))],
        out_specs=[
            pl.BlockSpec((SUBLANES, tt), lambda p, t: (0, p * t)),
            pl.BlockSpec((SUBLANES, meta_w), lambda p, t: (0, 0)),
        ],
        out_shape=[
            jax.ShapeDtypeStruct((SUBLANES, n), I32),
            jax.ShapeDtypeStruct((SUBLANES, meta_w), I32),
        ],
        scratch_shapes=[pltpu.VMEM((N_EXPERTS, LANES), F32), pltpu.VMEM((N_EXPERTS, LANES), F32)],
        compiler_params=_cparams(("arbitrary", "arbitrary")),
    )(eid)


def _dispatch_kernel(meta_ref, dest_ref, h2_ref, xs_hbm, zero_ref, zsem, sem, *, n_blocks):
    i = pl.program_id(0)
    tt = dest_ref.shape[0] // TOP_K
    blk_rows = FFN_BLOCK * ROW_CHUNKS

    def zero_copy(blk):
        start = pl.multiple_of(blk * blk_rows, blk_rows)
        return pltpu.make_async_copy(zero_ref, xs_hbm.at[pl.ds(start, blk_rows)], zsem)

    @pl.when(i == 0)
    def _():
        zero_ref[...] = jnp.zeros_like(zero_ref)
        for e in range(N_EXPERTS):
            @pl.when(meta_ref[META_COUNT, e] > 0)
            def _():
                blk = meta_ref[META_PAD_END, e] // FFN_BLOCK - 1
                cp = zero_copy(blk)
                cp.start()
                cp.wait()

        def tail(blk, carry):
            cp = zero_copy(blk)
            cp.start()
            cp.wait()
            return carry
        lax.fori_loop(meta_ref[META_N_USED, 0], n_blocks, tail, 0)

    def row_copy(t, slot):
        src = pl.multiple_of(t * ROW_CHUNKS, ROW_CHUNKS)
        dst = pl.multiple_of(slot * ROW_CHUNKS, ROW_CHUNKS)
        return pltpu.make_async_copy(h2_ref.at[pl.ds(src, ROW_CHUNKS)], xs_hbm.at[pl.ds(dst, ROW_CHUNKS)], sem)

    def issue(g, carry):
        for u in range(DMA_ISSUE_UNROLL):
            t = g * DMA_ISSUE_UNROLL + u
            for k in range(TOP_K):
                row_copy(t, dest_ref[TOP_K * t + k]).start(priority=k)
        return carry
    lax.fori_loop(0, tt // DMA_ISSUE_UNROLL, issue, 0)

    def drain(g, carry):
        for _ in range(DMA_DRAIN_UNROLL):
            row_copy(0, 0).wait()
        return carry
    lax.fori_loop(0, TOP_K * tt // DMA_DRAIN_UNROLL, drain, 0)


def _dispatch(meta, dest_flat, h2rows, n_blocks, tt=512):
    n = dest_flat.shape[0] // TOP_K
    p_rows = n_blocks * FFN_BLOCK * ROW_CHUNKS
    return pl.pallas_call(
        functools.partial(_dispatch_kernel, n_blocks=n_blocks),
        grid_spec=pltpu.PrefetchScalarGridSpec(
            num_scalar_prefetch=1,
            grid=(n // tt,),
            in_specs=[
                pl.BlockSpec((TOP_K * tt,), lambda i, m: (i,), memory_space=pltpu.SMEM),
                pl.BlockSpec((tt * ROW_CHUNKS, LANES), lambda i, m: (i, 0)),
            ],
            out_specs=pl.BlockSpec(memory_space=pl.ANY),
            scratch_shapes=[
                pltpu.VMEM((FFN_BLOCK * ROW_CHUNKS, LANES), F32),
                pltpu.SemaphoreType.DMA,
                pltpu.SemaphoreType.DMA,
            ],
        ),
        out_shape=jax.ShapeDtypeStruct((p_rows, LANES), F32),
        compiler_params=_cparams(("arbitrary",)),
    )(meta, dest_flat, h2rows)


def _ffn_kernel(meta_ref, x_ref, wg_ref, wu_ref, wd_ref, o_ref, wgu_bf_ref, wd_bf_ref):
    i = pl.program_id(0)
    used = i < meta_ref[META_N_USED, 0]
    expert = meta_ref[META_BLOCK_E, i]
    prev_expert = meta_ref[META_BLOCK_E, jnp.maximum(i - 1, 0)]

    @pl.when((i == 0) | (expert != prev_expert))
    def _():
        wgu_bf_ref[:, 0:D_EXPERT] = wg_ref[...].astype(BF16)
        wgu_bf_ref[:, D_EXPERT:] = wu_ref[...].astype(BF16)
        wd_bf_ref[...] = wd_ref[...].astype(BF16)

    @pl.when(used)
    def _():
        x = jnp.concatenate([x_ref[pl.ds(c, FFN_BLOCK, stride=ROW_CHUNKS), :] for c in range(ROW_CHUNKS)],
                            axis=1).astype(BF16)
        gu = jnp.dot(x, wgu_bf_ref[...], preferred_element_type=F32)
        g, u = gu[:, 0:D_EXPERT], gu[:, D_EXPERT:]
        hid = (g * jax.nn.sigmoid(g) * u).astype(BF16)
        y = jnp.dot(hid, wd_bf_ref[...], preferred_element_type=F32)
        for c in range(ROW_CHUNKS):
            o_ref[pl.ds(c, FFN_BLOCK, stride=ROW_CHUNKS), :] = y[:, c * LANES:(c + 1) * LANES]

    @pl.when(jnp.logical_not(used))
    def _():
        o_ref[...] = jnp.zeros_like(o_ref)


def _ffn(meta, xs, w_gate, w_up, w_down, n_blocks):
    blk_rows = FFN_BLOCK * ROW_CHUNKS
    row_spec = pl.BlockSpec((blk_rows, LANES), lambda i, m: (i, 0))
    return pl.pallas_call(
        _ffn_kernel,
        grid_spec=pltpu.PrefetchScalarGridSpec(
            num_scalar_prefetch=1,
            grid=(n_blocks,),
            in_specs=[
                row_spec,
                pl.BlockSpec((None, D_MODEL, D_EXPERT), lambda i, m: (m[META_BLOCK_E, i], 0, 0)),
                pl.BlockSpec((None, D_MODEL, D_EXPERT), lambda i, m: (m[META_BLOCK_E, i], 0, 0)),
                pl.BlockSpec((None, D_EXPERT, D_MODEL), lambda i, m: (m[META_BLOCK_E, i], 0, 0)),
            ],
            out_specs=row_spec,
            scratch_shapes=[pltpu.VMEM((D_MODEL, 2 * D_EXPERT), BF16), pltpu.VMEM((D_EXPERT, D_MODEL), BF16)],
        ),
        out_shape=jax.ShapeDtypeStruct(xs.shape, F32),
        compiler_params=_cparams(("arbitrary",)),
    )(meta, xs, w_gate, w_up, w_down)


def _combine_kernel(dest_ref, next_dest_ref, wts_ref, x1_ref, g2_ref, fg_ref, yb_hbm, o_ref, buf_ref, sems):
    i = pl.program_id(0)
    last = pl.num_programs(0) - 1
    tt = x1_ref.shape[0]
    slot = i % 2

    def row_copy(buf_slot, k, t, row):
        src = pl.multiple_of(row * ROW_CHUNKS, ROW_CHUNKS)
        dst = pl.multiple_of(t * ROW_CHUNKS, ROW_CHUNKS)
        return pltpu.make_async_copy(yb_hbm.at[pl.ds(src, ROW_CHUNKS)],
                                     buf_ref.at[buf_slot, k, pl.ds(dst, ROW_CHUNKS)], sems.at[buf_slot])

    def issue_tile(idx_ref, buf_slot):
        def body(g, carry):
            for u in range(DMA_ISSUE_UNROLL):
                t = g * DMA_ISSUE_UNROLL + u
                for k in range(TOP_K):
                    row_copy(buf_slot, k, t, idx_ref[TOP_K * t + k]).start(priority=k)
            return carry
        lax.fori_loop(0, tt // DMA_ISSUE_UNROLL, body, 0)

    @pl.when(i == 0)
    def _():
        issue_tile(dest_ref, 0)

    @pl.when(i < last)
    def _():
        issue_tile(next_dest_ref, 1 - slot)

    def drain(g, carry):
        for _ in range(DMA_DRAIN_UNROLL):
            row_copy(slot, 0, 0, 0).wait()
        return carry
    lax.fori_loop(0, TOP_K * tt // DMA_DRAIN_UNROLL, drain, 0)

    eye = jnp.where(lax.broadcasted_iota(I32, (SUBLANES, LANES), 0) == lax.broadcasted_iota(I32, (SUBLANES, LANES), 1),
                    1.0, 0.0)
    wcol = lax.dot_general(wts_ref[...], eye, (((0,), (0,)), ((), ())), precision=HIGHEST,
                           preferred_element_type=F32)
    ya = jnp.concatenate([buf_ref[slot, 0, pl.ds(c, tt, stride=ROW_CHUNKS), :] for c in range(ROW_CHUNKS)], axis=1)
    yb = jnp.concatenate([buf_ref[slot, 1, pl.ds(c, tt, stride=ROW_CHUNKS), :] for c in range(ROW_CHUNKS)], axis=1)
    y = ya * wcol[:, 0:1] + yb * wcol[:, 1:2]
    x2 = x1_ref[...] + g2_ref[...] * y
    o_ref[...] = x2 * lax.rsqrt(jnp.mean(x2 * x2, axis=-1, keepdims=True) + RMS_EPS) * fg_ref[...]


def _combine(dest_flat, wts, x1, mod4, final_g, yb, seq, tt=256):
    n = x1.shape[0]
    spt = seq // tt
    steps = n // tt
    return pl.pallas_call(
        _combine_kernel,
        grid=(steps,),
        in_specs=[
            pl.BlockSpec((TOP_K * tt,), lambda i: (i,), memory_space=pltpu.SMEM),
            pl.BlockSpec((TOP_K * tt,), lambda i: (jnp.minimum(i + 1, steps - 1),), memory_space=pltpu.SMEM),
            pl.BlockSpec((SUBLANES, tt), lambda i: (0, i)),
            pl.BlockSpec((tt, D_MODEL), lambda i: (i, 0)),
            pl.BlockSpec((None, None, 1, D_MODEL), lambda i: (5, i // spt, 0, 0)),
            pl.BlockSpec((1, D_MODEL), lambda i: (0, 0)),
            pl.BlockSpec(memory_space=pl.ANY),
        ],
        out_specs=pl.BlockSpec((tt, D_MODEL), lambda i: (i, 0)),
        out_shape=jax.ShapeDtypeStruct((n, D_MODEL), F32),
        scratch_shapes=[pltpu.VMEM((2, 2, tt * ROW_CHUNKS, LANES), F32), pltpu.SemaphoreType.DMA((2,))],
        compiler_params=_cparams(("arbitrary",)),
    )(dest_flat, dest_flat, wts, x1, mod4, final_g, yb)


def kernel(x, c, w_ada, b_ada, norm1_g, w_in, lambda_q1, lambda_k1, lambda_q2, lambda_k2, subln_g, w_pool,
           pool_scale, w_out, norm2_g, w_router_group, b_router_group, w_router_expert, b_router_expert,
           w_gate, w_up, w_down, final_g):
    bsz, seq, d = x.shape
    n = bsz * seq
    x2 = x.reshape(n, d)

    mod = _ada(c, w_ada[0], b_ada[0])
    mod4 = mod.reshape(mod.shape[0], bsz, 1, d)

    q, k, v, u = _inproj(x2, mod4, norm1_g, w_in[0].astype(BF16), seq)
    a_out = _attention(q, k, v, lambda_q1, lambda_k1, lambda_q2, lambda_k2, subln_g, bsz, seq)

    pad_rows = ROUTER_ROWS - N_EXPERTS - N_GROUPS
    wr = jnp.concatenate([w_router_expert[0].T, w_router_group[0].T, jnp.zeros((pad_rows, d), F32)], axis=0)
    wr_hi = wr.astype(BF16)
    wr_lo = (wr - wr_hi.astype(F32)).astype(BF16)
    wr = jnp.concatenate([wr_hi, wr_lo], axis=0)
    br = jnp.concatenate([b_router_expert[0], b_router_group[0], jnp.zeros((pad_rows,), F32)])
    br = jnp.broadcast_to(br[:, None], (br.shape[0], LANES))
    x1, h2rows, eid, wts = _mix(a_out, u, x2, mod4, w_pool[0].astype(BF16),
                                pool_scale[0].reshape(len(POOL_WINDOWS), 1, POOL_GROUP_DIM),
                                w_out[0].astype(BF16), norm2_g, wr, br, bsz, seq)

    n_blocks = pl.cdiv(2 * n, FFN_BLOCK) + N_EXPERTS
    dest, meta = _plan(eid, n_blocks)
    dest_flat = dest[0:TOP_K].T.reshape(-1)
    xs = _dispatch(meta, dest_flat, h2rows, n_blocks)
    yb = _ffn(meta, xs, w_gate[0], w_up[0], w_down[0], n_blocks)
    out = _combine(dest_flat, wts, x1, mod4, final_g.reshape(1, d), yb, seq)
    return out.reshape(bsz, seq, d)
```

```python
import functools
import math

import jax
import jax.numpy as jnp
from jax import lax
from jax.experimental import pallas as pl
from jax.experimental.pallas import tpu as pltpu

F32 = jnp.float32
BF16 = jnp.bfloat16
I32 = jnp.int32
HIGHEST = lax.Precision.HIGHEST

D_MODEL = 1024
ATTN_WIDTH = 512
N_HEADS = 4
HEAD_DIM = 64
VALUE_DIM = 128
ROT_DIM = 16
ROPE_THETA = 500000.0
POOL_WINDOWS = (2, 4, 8, 16)
POOL_GROUP_DIM = 128
N_GROUPS = 4
EXPERTS_PER_GROUP = 8
N_EXPERTS = 32
TOP_K = 2
D_EXPERT = 512
RMS_EPS = 1e-6
LAMBDA_INIT = 0.8 - 0.6 * math.exp(-0.3 * 0)

LANES = 128
SUBLANES = 8
ROW_CHUNKS = D_MODEL // LANES

POOL_HALO = 8
FFN_BLOCK = 256
DMA_ISSUE_UNROLL = 8
DMA_DRAIN_UNROLL = 16
ROUTER_ROWS = 48
VMEM_LIMIT = 56 * 1024 * 1024


def _cparams(sem, vmem=VMEM_LIMIT):
    return pltpu.CompilerParams(dimension_semantics=sem, vmem_limit_bytes=vmem)


def _ada_kernel(c_ref, w_ref, b_ref, o_ref):
    c = c_ref[...]
    act = c * jax.nn.sigmoid(c)
    o_ref[...] = jnp.dot(act, w_ref[...], precision=HIGHEST, preferred_element_type=F32) + b_ref[...]


def _ada(c, w_ada, b_ada):
    bsz = c.shape[0]
    n_mod = w_ada.shape[1] // D_MODEL
    return pl.pallas_call(
        _ada_kernel,
        grid=(n_mod,),
        in_specs=[
            pl.BlockSpec((bsz, D_MODEL), lambda j: (0, 0)),
            pl.BlockSpec((D_MODEL, D_MODEL), lambda j: (0, j)),
            pl.BlockSpec((1, D_MODEL), lambda j: (0, j)),
        ],
        out_specs=pl.BlockSpec((None, bsz, D_MODEL), lambda j: (j, 0, 0)),
        out_shape=jax.ShapeDtypeStruct((n_mod, bsz, D_MODEL), F32),
        compiler_params=_cparams(("arbitrary",)),
    )(c, w_ada, b_ada.reshape(1, -1))


def _inproj_kernel(x_ref, sh_ref, sc_ref, g_ref, w_ref, cos_ref, sa_ref, sb_ref,
                   q_ref, k_ref, v_ref, u_ref):
    x = x_ref[...]
    y = x * lax.rsqrt(jnp.mean(x * x, axis=-1, keepdims=True) + RMS_EPS) * g_ref[...]
    h = (y * (1.0 + sc_ref[...]) + sh_ref[...]).astype(BF16)
    cos, sa, sb = cos_ref[...], sa_ref[...], sb_ref[...]
    for part, o_ref, scale in ((0, q_ref, HEAD_DIM ** -0.5 * math.log2(math.e)), (1, k_ref, None)):
        z = jnp.dot(h, w_ref[:, part * ATTN_WIDTH:(part + 1) * ATTN_WIDTH], preferred_element_type=F32)
        for hd in range(N_HEADS):
            zc = z[:, hd * LANES:(hd + 1) * LANES]
            r = zc * cos + pltpu.roll(zc, ROT_DIM // 2, 1) * sa + pltpu.roll(zc, LANES - ROT_DIM // 2, 1) * sb
            if scale is not None:
                r = r * scale
            o_ref[:, hd * LANES:(hd + 1) * LANES] = r.astype(BF16)
    v_ref[...] = jnp.dot(h, w_ref[:, 2 * ATTN_WIDTH:3 * ATTN_WIDTH], preferred_element_type=F32).astype(BF16)
    u_ref[...] = jnp.dot(h, w_ref[:, 3 * ATTN_WIDTH:], preferred_element_type=F32).astype(BF16)


def _rope_lane_tables(seq):
    pos = jnp.arange(seq, dtype=F32)
    inv_freq = ROPE_THETA ** (-jnp.arange(0, ROT_DIM, 2, dtype=F32) / ROT_DIM)
    ang = pos[:, None] * inv_freq[None, :]
    cos, sin = jnp.cos(ang), jnp.sin(ang)
    half = ROT_DIM // 2
    ones = jnp.ones((seq, HEAD_DIM - ROT_DIM), F32)
    zeros_h = jnp.zeros((seq, half), F32)
    zeros_r = jnp.zeros((seq, HEAD_DIM - ROT_DIM), F32)
    cos_t = jnp.concatenate([cos, cos, ones], axis=1)
    sa_t = jnp.concatenate([zeros_h, sin, zeros_r], axis=1)
    sb_t = jnp.concatenate([-sin, zeros_h, zeros_r], axis=1)
    rep = LANES // HEAD_DIM
    return jnp.tile(cos_t, (1, rep)), jnp.tile(sa_t, (1, rep)), jnp.tile(sb_t, (1, rep))


def _inproj(x2, mod4, norm1_g, w_in_bf, seq, tm=512):
    n = x2.shape[0]
    spt = seq // tm
    cos_t, sa_t, sb_t = _rope_lane_tables(seq)
    mod_spec = lambda which: pl.BlockSpec((None, None, 1, D_MODEL), lambda i: (which, i // spt, 0, 0))
    tab_spec = pl.BlockSpec((tm, LANES), lambda i: (i % spt, 0))
    out_spec = pl.BlockSpec((tm, ATTN_WIDTH), lambda i: (i, 0))
    out_sds = jax.ShapeDtypeStruct((n, ATTN_WIDTH), BF16)
    return pl.pallas_call(
        _inproj_kernel,
        grid=(n // tm,),
        in_specs=[
            pl.BlockSpec((tm, D_MODEL), lambda i: (i, 0)),
            mod_spec(0), mod_spec(1),
            pl.BlockSpec((1, D_MODEL), lambda i: (0, 0)),
            pl.BlockSpec((D_MODEL, 4 * ATTN_WIDTH), lambda i: (0, 0)),
            tab_spec, tab_spec, tab_spec,
        ],
        out_specs=[out_spec] * 4,
        out_shape=[out_sds] * 4,
        compiler_params=_cparams(("arbitrary",)),
    )(x2, mod4, mod4, norm1_g, w_in_bf, cos_t, sa_t, sb_t)


def _attn_kernel(lq1_ref, lk1_ref, lq2_ref, lk2_ref, sg_ref, q_ref, k_ref, v_ref, o_ref,
                 sa_ref, sb_ref, ma_ref, mb_ref, *, tq):
    lam = (jnp.exp(jnp.sum(lq1_ref[...] * lk1_ref[...], axis=-1, keepdims=True))
           - jnp.exp(jnp.sum(lq2_ref[...] * lk2_ref[...], axis=-1, keepdims=True)) + LAMBDA_INIT)
    n_tiles = q_ref.shape[0] // tq
    bufs = ((sa_ref, ma_ref), (sb_ref, mb_ref))

    def score_stage(j, s_ref, m_ref):
        q = q_ref[j * tq:(j + 1) * tq, :]
        lane = lax.broadcasted_iota(I32, q.shape, 1)
        zero = jnp.zeros_like(q)
        qq = jnp.concatenate([jnp.where(lane < HEAD_DIM, q, zero), jnp.where(lane >= HEAD_DIM, q, zero)], axis=0)
        s = lax.dot_general(qq, k_ref[...], (((1,), (1,)), ((), ())), preferred_element_type=F32)
        s_ref[...] = s
        m_ref[...] = jnp.max(s, axis=-1, keepdims=True)

    def value_stage(j, s_ref, m_ref):
        p = jnp.exp2(s_ref[...] - m_ref[...])
        l = jnp.sum(p, axis=-1, keepdims=True)
        pb = p.astype(BF16)
        ratio = (lam * l[:tq] / l[tq:]).astype(BF16)
        o = jnp.dot(pb[:tq] - pb[tq:] * ratio, v_ref[...], preferred_element_type=F32) / l[:tq]
        o = o * lax.rsqrt(jnp.mean(o * o, axis=-1, keepdims=True) + RMS_EPS) * sg_ref[...]
        o_ref[j * tq:(j + 1) * tq, :] = (o * (1.0 - LAMBDA_INIT)).astype(BF16)

    score_stage(0, *bufs[0])
    for j in range(n_tiles):
        if j + 1 < n_tiles:
            score_stage(j + 1, *bufs[(j + 1) % 2])
        value_stage(j, *bufs[j % 2])


def _attention(q, k, v, lq1, lk1, lq2, lk2, subln_g, bsz, seq, tq=256):
    n = q.shape[0]
    vec_spec = lambda width: pl.BlockSpec((1, width), lambda b, h: (0, 0))
    head_spec = pl.BlockSpec((seq, LANES), lambda b, h: (b, h))
    return pl.pallas_call(
        functools.partial(_attn_kernel, tq=tq),
        grid=(bsz, N_HEADS),
        in_specs=[vec_spec(HEAD_DIM)] * 4 + [vec_spec(VALUE_DIM), head_spec, head_spec, head_spec],
        out_specs=head_spec,
        out_shape=jax.ShapeDtypeStruct((n, ATTN_WIDTH), BF16),
        scratch_shapes=[pltpu.VMEM((2 * tq, seq), F32), pltpu.VMEM((2 * tq, seq), F32),
                        pltpu.VMEM((2 * tq, 1), F32), pltpu.VMEM((2 * tq, 1), F32)],
        compiler_params=_cparams(("arbitrary",) * 2),
    )(lq1, lk1, lq2, lk2, subln_g, q, k, v)


def _mix_kernel(a_ref, u_ref, x_ref, g1_ref, sh2_ref, sc2_ref, wp_ref, ps_ref, wo_ref, n2_ref,
                wr_ref, br_ref, x1_ref, h2_ref, eid_ref, wts_ref, up_ref, *, seq):
    j = pl.program_id(1)
    tm = x_ref.shape[0]
    pool_w = u_ref.shape[1]

    @pl.when(j == 0)
    def _():
        up_ref[0:POOL_HALO, :] = jnp.zeros((POOL_HALO, pool_w), F32)
        up_ref[POOL_HALO + seq:, :] = jnp.zeros((POOL_HALO, pool_w), F32)
        up_ref[POOL_HALO:POOL_HALO + seq, :] = u_ref[...].astype(F32)

    r0 = pl.multiple_of(j * tm, tm)
    pos = r0 + lax.broadcasted_iota(I32, (tm, 1), 0)
    heads = [a_ref[...]]
    for g, win in enumerate(POOL_WINDOWS):
        left = win // 2
        right = win - 1 - left
        cols = slice(g * POOL_GROUP_DIM, (g + 1) * POOL_GROUP_DIM)
        win_rows = up_ref[pl.ds(r0, tm + 2 * POOL_HALO), cols]
        tot, span = win_rows, 1
        while span < win:
            tot = tot[:tot.shape[0] - span] + tot[span:]
            span *= 2
        tot = tot[POOL_HALO - left:POOL_HALO - left + tm]
        cnt = (jnp.minimum(pos + (right + 1), seq) - jnp.maximum(pos - left, 0)).astype(F32)
        d = tot / cnt - win_rows[POOL_HALO:POOL_HALO + tm]
        yg = jnp.dot(d.astype(BF16), wp_ref[g], preferred_element_type=F32) * ps_ref[g]
        heads.append(yg.astype(BF16))
    mix = jnp.dot(jnp.concatenate(heads, axis=1), wo_ref[...], preferred_element_type=F32)
    x1 = x_ref[...] + g1_ref[...] * mix
    x1_ref[...] = x1
    y = x1 * lax.rsqrt(jnp.mean(x1 * x1, axis=-1, keepdims=True) + RMS_EPS) * n2_ref[...]
    h2 = y * (1.0 + sc2_ref[...]) + sh2_ref[...]
    for c in range(ROW_CHUNKS):
        h2_ref[pl.ds(c, tm, stride=ROW_CHUNKS), :] = h2[:, c * LANES:(c + 1) * LANES]

    n_rt = br_ref.shape[0]
    nt_dims = (((1,), (1,)), ((), ()))
    h_hi = h2.astype(BF16)
    h_lo = (h2 - h_hi.astype(F32)).astype(BF16)
    lt2 = lax.dot_general(wr_ref[...], h_hi, nt_dims, preferred_element_type=F32)
    lt = (lt2[:n_rt] + lt2[n_rt:] + lax.dot_general(wr_ref[0:n_rt, :], h_lo, nt_dims, preferred_element_type=F32)
          + br_ref[:, 0:1])
    gl = [lt[N_EXPERTS + i:N_EXPERTS + i + 1, :] for i in range(N_GROUPS)]
    gmax = jnp.maximum(jnp.maximum(gl[0], gl[1]), jnp.maximum(gl[2], gl[3]))
    gsel = jnp.where(gl[0] == gmax, 0, jnp.where(gl[1] == gmax, 1, jnp.where(gl[2] == gmax, 2, 3)))
    gden = sum(jnp.exp(g_ - gmax) for g_ in gl)
    e_in = lt[3 * EXPERTS_PER_GROUP:4 * EXPERTS_PER_GROUP, :]
    for i in (2, 1, 0):
        e_in = jnp.where(gsel == i, lt[i * EXPERTS_PER_GROUP:(i + 1) * EXPERTS_PER_GROUP, :], e_in)
    sub = lax.broadcasted_iota(I32, e_in.shape, 0)
    m1 = jnp.max(e_in, axis=0, keepdims=True)
    i1 = jnp.min(jnp.where(e_in == m1, sub, EXPERTS_PER_GROUP), axis=0, keepdims=True)
    rest = jnp.where(sub == i1, -jnp.inf, e_in)
    m2 = jnp.max(rest, axis=0, keepdims=True)
    i2 = jnp.min(jnp.where(rest == m2, sub, EXPERTS_PER_GROUP), axis=0, keepdims=True)
    p2 = jnp.exp(m2 - m1)
    w1 = 1.0 / ((1.0 + p2) * gden)
    w2 = p2 * w1
    row = lax.broadcasted_iota(I32, (SUBLANES, tm), 0)
    base = gsel * EXPERTS_PER_GROUP
    eid_ref[...] = jnp.where(row == 0, base + i1, jnp.where(row == 1, base + i2, 0))
    wts_ref[...] = jnp.where(row == 0, w1, jnp.where(row == 1, w2, 0.0))


def _mix(a_out, u, x2, mod4, w_pool_bf, pool_scale, w_out_bf, norm2_g, wr, br, bsz, seq, tm=512):
    n = x2.shape[0]
    spt = seq // tm
    mod_spec = lambda which: pl.BlockSpec((None, None, 1, D_MODEL), lambda b, j: (which, b, 0, 0))
    tok_spec = lambda width: pl.BlockSpec((tm, width), lambda b, j: (b * spt + j, 0))
    lane_spec = pl.BlockSpec((SUBLANES, tm), lambda b, j: (0, b * spt + j))
    return pl.pallas_call(
        functools.partial(_mix_kernel, seq=seq),
        grid=(bsz, spt),
        in_specs=[
            tok_spec(ATTN_WIDTH),
            pl.BlockSpec((seq, ATTN_WIDTH), lambda b, j: (b, 0)),
            tok_spec(D_MODEL),
            mod_spec(2), mod_spec(3), mod_spec(4),
            pl.BlockSpec((len(POOL_WINDOWS), POOL_GROUP_DIM, POOL_GROUP_DIM), lambda b, j: (0, 0, 0)),
            pl.BlockSpec((len(POOL_WINDOWS), 1, POOL_GROUP_DIM), lambda b, j: (0, 0, 0)),
            pl.BlockSpec((D_MODEL, D_MODEL), lambda b, j: (0, 0)),
            pl.BlockSpec((1, D_MODEL), lambda b, j: (0, 0)),
            pl.BlockSpec((2 * ROUTER_ROWS, D_MODEL), lambda b, j: (0, 0)),
            pl.BlockSpec((ROUTER_ROWS, LANES), lambda b, j: (0, 0)),
        ],
        out_specs=[
            tok_spec(D_MODEL),
            pl.BlockSpec((tm * ROW_CHUNKS, LANES), lambda b, j: (b * spt + j, 0)),
            lane_spec, lane_spec,
        ],
        out_shape=[
            jax.ShapeDtypeStruct((n, D_MODEL), F32),
            jax.ShapeDtypeStruct((n * ROW_CHUNKS, LANES), F32),
            jax.ShapeDtypeStruct((SUBLANES, n), I32),
            jax.ShapeDtypeStruct((SUBLANES, n), F32),
        ],
        scratch_shapes=[pltpu.VMEM((seq + 2 * POOL_HALO, ATTN_WIDTH), F32)],
        compiler_params=_cparams(("arbitrary", "arbitrary")),
    )(a_out, u, x2, mod4, mod4, mod4, w_pool_bf, pool_scale, w_out_bf, norm2_g, wr, br)


META_BLOCK_E, META_N_USED, META_PAD_END, META_COUNT = 0, 1, 2, 3


def _plan_kernel(eid_ref, dest_ref, meta_ref, cnt_ref, base_ref, *, n_blocks, meta_w, chunk):
    phase = pl.program_id(0)
    t = pl.program_id(1)
    nt = pl.num_programs(1)
    n_chunks = eid_ref.shape[1] // chunk
    sub = lax.broadcasted_iota(I32, (N_EXPERTS, chunk), 0)

    def one_hots(c):
        oh0 = sub == eid_ref[0:1, c * chunk:(c + 1) * chunk]
        oh1 = sub == eid_ref[1:2, c * chunk:(c + 1) * chunk]
        return oh0, oh1, jnp.where(oh0 | oh1, 1.0, 0.0)

    @pl.when((phase == 0) & (t == 0))
    def _():
        cnt_ref[...] = jnp.zeros_like(cnt_ref)

    @pl.when(phase == 0)
    def _():
        cnt_ref[...] += sum(jnp.sum(one_hots(c)[2], axis=1, keepdims=True) for c in range(n_chunks))

    @pl.when((phase == 0) & (t == nt - 1))
    def _():
        cnt = cnt_ref[...]
        padded = jnp.floor((cnt + (FFN_BLOCK - 1)) * (1.0 / FFN_BLOCK)) * FFN_BLOCK
        er = lax.broadcasted_iota(I32, (N_EXPERTS, N_EXPERTS), 0)
        ec = lax.broadcasted_iota(I32, (N_EXPERTS, N_EXPERTS), 1)
        tri = jnp.where(ec <= er, 1.0, 0.0)
        pad_end = jnp.dot(tri, padded, precision=HIGHEST, preferred_element_type=F32)
        base_ref[...] = pad_end - padded
        blk_row = (lax.broadcasted_iota(I32, (N_EXPERTS, meta_w), 1) * FFN_BLOCK).astype(F32)
        block_e = jnp.sum(jnp.where(pad_end[:, 0:1] <= blk_row, 1.0, 0.0), axis=0, keepdims=True)
        block_e = jnp.minimum(block_e, N_EXPERTS - 1.0)
        n_used = pad_end[N_EXPERTS - 1:N_EXPERTS, 0:1] * (1.0 / FFN_BLOCK)
        diag = (lax.broadcasted_iota(I32, (N_EXPERTS, LANES), 0)
                == lax.broadcasted_iota(I32, (N_EXPERTS, LANES), 1))
        pe_lane = jnp.sum(jnp.where(diag, pad_end, 0.0), axis=0, keepdims=True)
        cnt_lane = jnp.sum(jnp.where(diag, cnt, 0.0), axis=0, keepdims=True)
        zpad = jnp.zeros((1, meta_w - LANES), F32)
        row = lax.broadcasted_iota(I32, (SUBLANES, meta_w), 0)
        meta = jnp.where(row == META_BLOCK_E, block_e,
                         jnp.where(row == META_N_USED, n_used,
                                   jnp.where(row == META_PAD_END, jnp.concatenate([pe_lane, zpad], axis=1),
                                             jnp.where(row == META_COUNT, jnp.concatenate([cnt_lane, zpad], axis=1),
                                                       0.0))))
        meta_ref[...] = meta.astype(I32)

    @pl.when(phase == 1)
    def _():
        r = lax.broadcasted_iota(I32, (chunk, chunk), 0)
        c = lax.broadcasted_iota(I32, (chunk, chunk), 1)
        triu = jnp.where(r <= c, 1.0, 0.0).astype(BF16)
        row = lax.broadcasted_iota(I32, (SUBLANES, chunk), 0)
        base = base_ref[:, 0:1]
        for ci in range(n_chunks):
            oh0, oh1, oh = one_hots(ci)
            incl = jnp.dot(oh.astype(BF16), triu, preferred_element_type=F32)
            slot = base + incl - oh
            d0 = jnp.sum(jnp.where(oh0, slot, 0.0), axis=0, keepdims=True)
            d1 = jnp.sum(jnp.where(oh1, slot, 0.0), axis=0, keepdims=True)
            dest_ref[:, ci * chunk:(ci + 1) * chunk] = jnp.where(row == 0, d0, jnp.where(row == 1, d1, 0.0)).astype(I32)
            base = base + incl[:, chunk - 1:chunk]
        base_ref[...] = jnp.broadcast_to(base, base_ref.shape)


def _plan(eid, n_blocks, tt=2048, chunk=512):
    n = eid.shape[1]
    meta_w = pl.cdiv(n_blocks, LANES) * LANES
    return pl.pallas_call(
        functools.partial(_plan_kernel, n_blocks=n_blocks, meta_w=meta_w, chunk=chunk),
        grid=(2, n // tt),
        in_specs=[pl.BlockSpec((SUBLANES, tt), lambda p, t: (0, t))],
        out_specs=[
            pl.BlockSpec((SUBLANES, tt), lambda p, t: (0, p * t)),
            pl.BlockSpec((SUBLANES, meta_w), lambda p, t: (0, 0)),
        ],
        out_shape=[
            jax.ShapeDtypeStruct((SUBLANES, n), I32),
            jax.ShapeDtypeStruct((SUBLANES, meta_w), I32),
        ],
        scratch_shapes=[pltpu.VMEM((N_EXPERTS, LANES), F32), pltpu.VMEM((N_EXPERTS, LANES), F32)],
        compiler_params=_cparams(("arbitrary", "arbitrary")),
    )(eid)


def _dispatch_kernel(meta_ref, dest_ref, h2_ref, xs_hbm, zero_ref, zsem, sem, *, n_blocks):
    i = pl.program_id(0)
    tt = dest_ref.shape[0] // TOP_K
    blk_rows = FFN_BLOCK * ROW_CHUNKS

    def zero_copy(blk):
        start = pl.multiple_of(blk * blk_rows, blk_rows)
        return pltpu.make_async_copy(zero_ref, xs_hbm.at[pl.ds(start, blk_rows)], zsem)

    @pl.when(i == 0)
    def _():
        zero_ref[...] = jnp.zeros_like(zero_ref)
        for e in range(N_EXPERTS):
            @pl.when(meta_ref[META_COUNT, e] > 0)
            def _():
                blk = meta_ref[META_PAD_END, e] // FFN_BLOCK - 1
                cp = zero_copy(blk)
                cp.start()
                cp.wait()

        def tail(blk, carry):
            cp = zero_copy(blk)
            cp.start()
            cp.wait()
            return carry
        lax.fori_loop(meta_ref[META_N_USED, 0], n_blocks, tail, 0)

    def row_copy(t, slot):
        src = pl.multiple_of(t * ROW_CHUNKS, ROW_CHUNKS)
        dst = pl.multiple_of(slot * ROW_CHUNKS, ROW_CHUNKS)
        return pltpu.make_async_copy(h2_ref.at[pl.ds(src, ROW_CHUNKS)], xs_hbm.at[pl.ds(dst, ROW_CHUNKS)], sem)

    def issue(g, carry):
        for u in range(DMA_ISSUE_UNROLL):
            t = g * DMA_ISSUE_UNROLL + u
            for k in range(TOP_K):
                row_copy(t, dest_ref[TOP_K * t + k]).start(priority=k)
        return carry
    lax.fori_loop(0, tt // DMA_ISSUE_UNROLL, issue, 0)

    def drain(g, carry):
        for _ in range(DMA_DRAIN_UNROLL):
            row_copy(0, 0).wait()
        return carry
    lax.fori_loop(0, TOP_K * tt // DMA_DRAIN_UNROLL, drain, 0)


def _dispatch(meta, dest_flat, h2rows, n_blocks, tt=512):
    n = dest_flat.shape[0] // TOP_K
    p_rows = n_blocks * FFN_BLOCK * ROW_CHUNKS
    return pl.pallas_call(
        functools.partial(_dispatch_kernel, n_blocks=n_blocks),
        grid_spec=pltpu.PrefetchScalarGridSpec(
            num_scalar_prefetch=1,
            grid=(n // tt,),
            in_specs=[
                pl.BlockSpec((TOP_K * tt,), lambda i, m: (i,), memory_space=pltpu.SMEM),
                pl.BlockSpec((tt * ROW_CHUNKS, LANES), lambda i, m: (i, 0)),
            ],
            out_specs=pl.BlockSpec(memory_space=pl.ANY),
            scratch_shapes=[
                pltpu.VMEM((FFN_BLOCK * ROW_CHUNKS, LANES), F32),
                pltpu.SemaphoreType.DMA,
                pltpu.SemaphoreType.DMA,
            ],
        ),
        out_shape=jax.ShapeDtypeStruct((p_rows, LANES), F32),
        compiler_params=_cparams(("arbitrary",)),
    )(meta, dest_flat, h2rows)


def _ffn_kernel(meta_ref, x_ref, wg_hbm, wu_hbm, wd_hbm, o_ref,
                wg_st_ref, wu_st_ref, wd_st_ref, wgu_bf_ref, wd_bf_ref, slot_ref, sems):
    i = pl.program_id(0)
    n_used = meta_ref[META_N_USED, 0]
    used = i < n_used
    expert = meta_ref[META_BLOCK_E, i]
    prev_expert = meta_ref[META_BLOCK_E, jnp.maximum(i - 1, 0)]

    def weight_copies(e, slot):
        return (pltpu.make_async_copy(wg_hbm.at[e], wg_st_ref.at[slot], sems.at[slot, 0]),
                pltpu.make_async_copy(wu_hbm.at[e], wu_st_ref.at[slot], sems.at[slot, 1]),
                pltpu.make_async_copy(wd_hbm.at[e], wd_st_ref.at[slot], sems.at[slot, 2]))

    @pl.when(i == 0)
    def _():
        slot_ref[0] = 0
        for cp in weight_copies(expert, 0):
            cp.start()

    @pl.when(used & ((i == 0) | (expert != prev_expert)))
    def _():
        slot = slot_ref[0]
        for cp in weight_copies(expert, slot):
            cp.wait()
        wgu_bf_ref[:, 0:D_EXPERT] = wg_st_ref[slot].astype(BF16)
        wgu_bf_ref[:, D_EXPERT:] = wu_st_ref[slot].astype(BF16)
        wd_bf_ref[...] = wd_st_ref[slot].astype(BF16)
        next_first = meta_ref[META_PAD_END, expert] // FFN_BLOCK

        @pl.when(next_first < n_used)
        def _():
            for cp in weight_copies(meta_ref[META_BLOCK_E, next_first], 1 - slot):
                cp.start()
        slot_ref[0] = 1 - slot

    @pl.when(used)
    def _():
        x = jnp.concatenate([x_ref[pl.ds(c, FFN_BLOCK, stride=ROW_CHUNKS), :] for c in range(ROW_CHUNKS)],
                            axis=1).astype(BF16)
        gu = jnp.dot(x, wgu_bf_ref[...], preferred_element_type=F32)
        g, u = gu[:, 0:D_EXPERT], gu[:, D_EXPERT:]
        hid = (g * jax.nn.sigmoid(g) * u).astype(BF16)
        y = jnp.dot(hid, wd_bf_ref[...], preferred_element_type=F32)
        for c in range(ROW_CHUNKS):
            o_ref[pl.ds(c, FFN_BLOCK, stride=ROW_CHUNKS), :] = y[:, c * LANES:(c + 1) * LANES]

    @pl.when(jnp.logical_not(used))
    def _():
        o_ref[...] = jnp.zeros_like(o_ref)


def _ffn(meta, xs, w_gate, w_up, w_down, n_blocks):
    blk_rows = FFN_BLOCK * ROW_CHUNKS
    row_spec = pl.BlockSpec((blk_rows, LANES), lambda i, m: (i, 0))
    return pl.pallas_call(
        _ffn_kernel,
        grid_spec=pltpu.PrefetchScalarGridSpec(
            num_scalar_prefetch=1,
            grid=(n_blocks,),
            in_specs=[
                row_spec,
                pl.BlockSpec(memory_space=pl.ANY),
                pl.BlockSpec(memory_space=pl.ANY),
                pl.BlockSpec(memory_space=pl.ANY),
            ],
            out_specs=row_spec,
            scratch_shapes=[
                pltpu.VMEM((2, D_MODEL, D_EXPERT), F32),
                pltpu.VMEM((2, D_MODEL, D_EXPERT), F32),
                pltpu.VMEM((2, D_EXPERT, D_MODEL), F32),
                pltpu.VMEM((D_MODEL, 2 * D_EXPERT), BF16),
                pltpu.VMEM((D_EXPERT, D_MODEL), BF16),
                pltpu.SMEM((1,), I32),
                pltpu.SemaphoreType.DMA((2, 3)),
            ],
        ),
        out_shape=jax.ShapeDtypeStruct(xs.shape, F32),
        compiler_params=_cparams(("arbitrary",)),
    )(meta, xs, w_gate, w_up, w_down)


def _combine_kernel(dest_ref, next_dest_ref, wts_ref, x1_ref, g2_ref, fg_ref, yb_hbm, o_ref, buf_ref, sems):
    i = pl.program_id(0)
    last = pl.num_programs(0) - 1
    tt = x1_ref.shape[0]
    slot = i % 2

    def row_copy(buf_slot, k, t, row):
        src = pl.multiple_of(row * ROW_CHUNKS, ROW_CHUNKS)
        dst = pl.multiple_of(t * ROW_CHUNKS, ROW_CHUNKS)
        return pltpu.make_async_copy(yb_hbm.at[pl.ds(src, ROW_CHUNKS)],
                                     buf_ref.at[buf_slot, k, pl.ds(dst, ROW_CHUNKS)], sems.at[buf_slot])

    def issue_tile(idx_ref, buf_slot):
        def body(g, carry):
            for u in range(DMA_ISSUE_UNROLL):
                t = g * DMA_ISSUE_UNROLL + u
                for k in range(TOP_K):
                    row_copy(buf_slot, k, t, idx_ref[TOP_K * t + k]).start(priority=k)
            return carry
        lax.fori_loop(0, tt // DMA_ISSUE_UNROLL, body, 0)

    @pl.when(i == 0)
    def _():
        issue_tile(dest_ref, 0)

    @pl.when(i < last)
    def _():
        issue_tile(next_dest_ref, 1 - slot)

    def drain(g, carry):
        for _ in range(DMA_DRAIN_UNROLL):
            row_copy(slot, 0, 0, 0).wait()
        return carry
    lax.fori_loop(0, TOP_K * tt // DMA_DRAIN_UNROLL, drain, 0)

    eye = jnp.where(lax.broadcasted_iota(I32, (SUBLANES, LANES), 0) == lax.broadcasted_iota(I32, (SUBLANES, LANES), 1),
                    1.0, 0.0)
    wcol = lax.dot_general(wts_ref[...], eye, (((0,), (0,)), ((), ())), precision=HIGHEST,
                           preferred_element_type=F32)
    ya = jnp.concatenate([buf_ref[slot, 0, pl.ds(c, tt, stride=ROW_CHUNKS), :] for c in range(ROW_CHUNKS)], axis=1)
    yb = jnp.concatenate([buf_ref[slot, 1, pl.ds(c, tt, stride=ROW_CHUNKS), :] for c in range(ROW_CHUNKS)], axis=1)
    y = ya * wcol[:, 0:1] + yb * wcol[:, 1:2]
    x2 = x1_ref[...] + g2_ref[...] * y
    o_ref[...] = x2 * lax.rsqrt(jnp.mean(x2 * x2, axis=-1, keepdims=True) + RMS_EPS) * fg_ref[...]


def _combine(dest_flat, wts, x1, mod4, final_g, yb, seq, tt=256):
    n = x1.shape[0]
    spt = seq // tt
    steps = n // tt
    return pl.pallas_call(
        _combine_kernel,
        grid=(steps,),
        in_specs=[
            pl.BlockSpec((TOP_K * tt,), lambda i: (i,), memory_space=pltpu.SMEM),
            pl.BlockSpec((TOP_K * tt,), lambda i: (jnp.minimum(i + 1, steps - 1),), memory_space=pltpu.SMEM),
            pl.BlockSpec((SUBLANES, tt), lambda i: (0, i)),
            pl.BlockSpec((tt, D_MODEL), lambda i: (i, 0)),
            pl.BlockSpec((None, None, 1, D_MODEL), lambda i: (5, i // spt, 0, 0)),
            pl.BlockSpec((1, D_MODEL), lambda i: (0, 0)),
            pl.BlockSpec(memory_space=pl.ANY),
        ],
        out_specs=pl.BlockSpec((tt, D_MODEL), lambda i: (i, 0)),
        out_shape=jax.ShapeDtypeStruct((n, D_MODEL), F32),
        scratch_shapes=[pltpu.VMEM((2, 2, tt * ROW_CHUNKS, LANES), F32), pltpu.SemaphoreType.DMA((2,))],
        compiler_params=_cparams(("arbitrary",)),
    )(dest_flat, dest_flat, wts, x1, mod4, final_g, yb)


def kernel(x, c, w_ada, b_ada, norm1_g, w_in, lambda_q1, lambda_k1, lambda_q2, lambda_k2, subln_g, w_pool,
           pool_scale, w_out, norm2_g, w_router_group, b_router_group, w_router_expert, b_router_expert,
           w_gate, w_up, w_down, final_g):
    bsz, seq, d = x.shape
    n = bsz * seq
    x2 = x.reshape(n, d)

    mod = _ada(c, w_ada[0], b_ada[0])
    mod4 = mod.reshape(mod.shape[0], bsz, 1, d)

    q, k, v, u = _inproj(x2, mod4, norm1_g, w_in[0].astype(BF16), seq)
    a_out = _attention(q, k, v, lambda_q1, lambda_k1, lambda_q2, lambda_k2, subln_g, bsz, seq)

    pad_rows = ROUTER_ROWS - N_EXPERTS - N_GROUPS
    wr = jnp.concatenate([w_router_expert[0].T, w_router_group[0].T, jnp.zeros((pad_rows, d), F32)], axis=0)
    wr_hi = wr.astype(BF16)
    wr_lo = (wr - wr_hi.astype(F32)).astype(BF16)
    wr = jnp.concatenate([wr_hi, wr_lo], axis=0)
    br = jnp.concatenate([b_router_expert[0], b_router_group[0], jnp.zeros((pad_rows,), F32)])
    br = jnp.broadcast_to(br[:, None], (br.shape[0], LANES))
    x1, h2rows, eid, wts = _mix(a_out, u, x2, mod4, w_pool[0].astype(BF16),
                                pool_scale[0].reshape(len(POOL_WINDOWS), 1, POOL_GROUP_DIM),
                                w_out[0].astype(BF16), norm2_g, wr, br, bsz, seq)

    n_blocks = pl.cdiv(2 * n, FFN_BLOCK) + N_EXPERTS
    dest, meta = _plan(eid, n_blocks)
    dest_flat = dest[0:TOP_K].T.reshape(-1)
    xs = _dispatch(meta, dest_flat, h2rows, n_blocks)
    yb = _ffn(meta, xs, w_gate[0], w_up[0], w_down[0], n_blocks)
    out = _combine(dest_flat, wts, x1, mod4, final_g.reshape(1, d), yb, seq)
    return out.reshape(bsz, seq, d)
```

```python
import functools
import math

import jax
import jax.numpy as jnp
from jax import lax
from jax.experimental import pallas as pl
from jax.experimental.pallas import tpu as pltpu

F32 = jnp.float32
BF16 = jnp.bfloat16
I32 = jnp.int32
HIGHEST = lax.Precision.HIGHEST

D_MODEL = 1024
ATTN_WIDTH = 512
N_HEADS = 4
HEAD_DIM = 64
VALUE_DIM = 128
ROT_DIM = 16
ROPE_THETA = 500000.0
POOL_WINDOWS = (2, 4, 8, 16)
POOL_GROUP_DIM = 128
N_GROUPS = 4
EXPERTS_PER_GROUP = 8
N_EXPERTS = 32
TOP_K = 2
D_EXPERT = 512
RMS_EPS = 1e-6
LAMBDA_INIT = 0.8 - 0.6 * math.exp(-0.3 * 0)

LANES = 128
SUBLANES = 8
ROW_CHUNKS = D_MODEL // LANES

POOL_HALO = 8
FFN_BLOCK = 256
FFN_BLOCKS_PER_STEP = 2
DMA_ISSUE_UNROLL = 8
DMA_DRAIN_UNROLL = 16
ROUTER_ROWS = 48
VMEM_LIMIT = 56 * 1024 * 1024


def _cparams(sem, vmem=VMEM_LIMIT):
    return pltpu.CompilerParams(dimension_semantics=sem, vmem_limit_bytes=vmem)


def _ada_kernel(c_ref, w_ref, b_ref, o_ref):
    c = c_ref[...]
    act = c * jax.nn.sigmoid(c)
    o_ref[...] = jnp.dot(act, w_ref[...], precision=HIGHEST, preferred_element_type=F32) + b_ref[...]


def _ada(c, w_ada, b_ada):
    bsz = c.shape[0]
    n_mod = w_ada.shape[1] // D_MODEL
    return pl.pallas_call(
        _ada_kernel,
        grid=(n_mod,),
        in_specs=[
            pl.BlockSpec((bsz, D_MODEL), lambda j: (0, 0)),
            pl.BlockSpec((D_MODEL, D_MODEL), lambda j: (0, j)),
            pl.BlockSpec((1, D_MODEL), lambda j: (0, j)),
        ],
        out_specs=pl.BlockSpec((None, bsz, D_MODEL), lambda j: (j, 0, 0)),
        out_shape=jax.ShapeDtypeStruct((n_mod, bsz, D_MODEL), F32),
        compiler_params=_cparams(("arbitrary",)),
    )(c, w_ada, b_ada.reshape(1, -1))


def _inproj_kernel(x_ref, sh_ref, sc_ref, g_ref, w_ref, cos_ref, sa_ref, sb_ref,
                   q_ref, k_ref, v_ref, u_ref):
    x = x_ref[...]
    y = x * lax.rsqrt(jnp.mean(x * x, axis=-1, keepdims=True) + RMS_EPS) * g_ref[...]
    h = (y * (1.0 + sc_ref[...]) + sh_ref[...]).astype(BF16)
    cos, sa, sb = cos_ref[...], sa_ref[...], sb_ref[...]
    for part, o_ref, scale in ((0, q_ref, HEAD_DIM ** -0.5 * math.log2(math.e)), (1, k_ref, None)):
        z = jnp.dot(h, w_ref[:, part * ATTN_WIDTH:(part + 1) * ATTN_WIDTH], preferred_element_type=F32)
        for hd in range(N_HEADS):
            zc = z[:, hd * LANES:(hd + 1) * LANES]
            r = zc * cos + pltpu.roll(zc, ROT_DIM // 2, 1) * sa + pltpu.roll(zc, LANES - ROT_DIM // 2, 1) * sb
            if scale is not None:
                r = r * scale
            o_ref[:, hd * LANES:(hd + 1) * LANES] = r.astype(BF16)
    v_ref[...] = jnp.dot(h, w_ref[:, 2 * ATTN_WIDTH:3 * ATTN_WIDTH], preferred_element_type=F32).astype(BF16)
    u_ref[...] = jnp.dot(h, w_ref[:, 3 * ATTN_WIDTH:], preferred_element_type=F32).astype(BF16)


def _rope_lane_tables(seq):
    pos = jnp.arange(seq, dtype=F32)
    inv_freq = ROPE_THETA ** (-jnp.arange(0, ROT_DIM, 2, dtype=F32) / ROT_DIM)
    ang = pos[:, None] * inv_freq[None, :]
    cos, sin = jnp.cos(ang), jnp.sin(ang)
    half = ROT_DIM // 2
    ones = jnp.ones((seq, HEAD_DIM - ROT_DIM), F32)
    zeros_h = jnp.zeros((seq, half), F32)
    zeros_r = jnp.zeros((seq, HEAD_DIM - ROT_DIM), F32)
    cos_t = jnp.concatenate([cos, cos, ones], axis=1)
    sa_t = jnp.concatenate([zeros_h, sin, zeros_r], axis=1)
    sb_t = jnp.concatenate([-sin, zeros_h, zeros_r], axis=1)
    rep = LANES // HEAD_DIM
    return jnp.tile(cos_t, (1, rep)), jnp.tile(sa_t, (1, rep)), jnp.tile(sb_t, (1, rep))


def _inproj(x2, mod4, norm1_g, w_in_bf, seq, tm=512):
    n = x2.shape[0]
    spt = seq // tm
    cos_t, sa_t, sb_t = _rope_lane_tables(seq)
    mod_spec = lambda which: pl.BlockSpec((None, None, 1, D_MODEL), lambda i: (which, i // spt, 0, 0))
    tab_spec = pl.BlockSpec((tm, LANES), lambda i: (i % spt, 0))
    out_spec = pl.BlockSpec((tm, ATTN_WIDTH), lambda i: (i, 0))
    out_sds = jax.ShapeDtypeStruct((n, ATTN_WIDTH), BF16)
    return pl.pallas_call(
        _inproj_kernel,
        grid=(n // tm,),
        in_specs=[
            pl.BlockSpec((tm, D_MODEL), lambda i: (i, 0)),
            mod_spec(0), mod_spec(1),
            pl.BlockSpec((1, D_MODEL), lambda i: (0, 0)),
            pl.BlockSpec((D_MODEL, 4 * ATTN_WIDTH), lambda i: (0, 0)),
            tab_spec, tab_spec, tab_spec,
        ],
        out_specs=[out_spec] * 4,
        out_shape=[out_sds] * 4,
        compiler_params=_cparams(("arbitrary",)),
    )(x2, mod4, mod4, norm1_g, w_in_bf, cos_t, sa_t, sb_t)


def _attn_kernel(lq1_ref, lk1_ref, lq2_ref, lk2_ref, sg_ref, q_ref, k_ref, v_ref, o_ref,
                 sa_ref, sb_ref, ma_ref, mb_ref, *, tq):
    lam = (jnp.exp(jnp.sum(lq1_ref[...] * lk1_ref[...], axis=-1, keepdims=True))
           - jnp.exp(jnp.sum(lq2_ref[...] * lk2_ref[...], axis=-1, keepdims=True)) + LAMBDA_INIT)
    n_tiles = q_ref.shape[0] // tq
    bufs = ((sa_ref, ma_ref), (sb_ref, mb_ref))

    def score_stage(j, s_ref, m_ref):
        q = q_ref[j * tq:(j + 1) * tq, :]
        lane = lax.broadcasted_iota(I32, q.shape, 1)
        zero = jnp.zeros_like(q)
        qq = jnp.concatenate([jnp.where(lane < HEAD_DIM, q, zero), jnp.where(lane >= HEAD_DIM, q, zero)], axis=0)
        s = lax.dot_general(qq, k_ref[...], (((1,), (1,)), ((), ())), preferred_element_type=F32)
        s_ref[...] = s
        m_ref[...] = jnp.max(s, axis=-1, keepdims=True)

    def value_stage(j, s_ref, m_ref):
        p = jnp.exp2(s_ref[...] - m_ref[...])
        l = jnp.sum(p, axis=-1, keepdims=True)
        pb = p.astype(BF16)
        ratio = (lam * l[:tq] / l[tq:]).astype(BF16)
        o = jnp.dot(pb[:tq] - pb[tq:] * ratio, v_ref[...], preferred_element_type=F32) / l[:tq]
        o = o * lax.rsqrt(jnp.mean(o * o, axis=-1, keepdims=True) + RMS_EPS) * sg_ref[...]
        o_ref[j * tq:(j + 1) * tq, :] = (o * (1.0 - LAMBDA_INIT)).astype(BF16)

    score_stage(0, *bufs[0])
    for j in range(n_tiles):
        if j + 1 < n_tiles:
            score_stage(j + 1, *bufs[(j + 1) % 2])
        value_stage(j, *bufs[j % 2])


def _attention(q, k, v, lq1, lk1, lq2, lk2, subln_g, bsz, seq, tq=256):
    n = q.shape[0]
    vec_spec = lambda width: pl.BlockSpec((1, width), lambda b, h: (0, 0))
    head_spec = pl.BlockSpec((seq, LANES), lambda b, h: (b, h))
    return pl.pallas_call(
        functools.partial(_attn_kernel, tq=tq),
        grid=(bsz, N_HEADS),
        in_specs=[vec_spec(HEAD_DIM)] * 4 + [vec_spec(VALUE_DIM), head_spec, head_spec, head_spec],
        out_specs=head_spec,
        out_shape=jax.ShapeDtypeStruct((n, ATTN_WIDTH), BF16),
        scratch_shapes=[pltpu.VMEM((2 * tq, seq), F32), pltpu.VMEM((2 * tq, seq), F32),
                        pltpu.VMEM((2 * tq, 1), F32), pltpu.VMEM((2 * tq, 1), F32)],
        compiler_params=_cparams(("arbitrary",) * 2),
    )(lq1, lk1, lq2, lk2, subln_g, q, k, v)


def _mix_kernel(a_ref, u_ref, x_ref, g1_ref, sh2_ref, sc2_ref, wp_ref, ps_ref, wo_ref, n2_ref,
                wr_ref, br_ref, x1_ref, h2_ref, eid_ref, wts_ref, up_ref, *, seq):
    j = pl.program_id(1)
    tm = x_ref.shape[0]
    pool_w = u_ref.shape[1]

    @pl.when(j == 0)
    def _():
        up_ref[0:POOL_HALO, :] = jnp.zeros((POOL_HALO, pool_w), F32)
        up_ref[POOL_HALO + seq:, :] = jnp.zeros((POOL_HALO, pool_w), F32)
        up_ref[POOL_HALO:POOL_HALO + seq, :] = u_ref[...].astype(F32)

    r0 = pl.multiple_of(j * tm, tm)
    pos = r0 + lax.broadcasted_iota(I32, (tm, 1), 0)
    heads = [a_ref[...]]
    for g, win in enumerate(POOL_WINDOWS):
        left = win // 2
        right = win - 1 - left
        cols = slice(g * POOL_GROUP_DIM, (g + 1) * POOL_GROUP_DIM)
        win_rows = up_ref[pl.ds(r0, tm + 2 * POOL_HALO), cols]
        tot, span = win_rows, 1
        while span < win:
            tot = tot[:tot.shape[0] - span] + tot[span:]
            span *= 2
        tot = tot[POOL_HALO - left:POOL_HALO - left + tm]
        cnt = (jnp.minimum(pos + (right + 1), seq) - jnp.maximum(pos - left, 0)).astype(F32)
        d = tot / cnt - win_rows[POOL_HALO:POOL_HALO + tm]
        yg = jnp.dot(d.astype(BF16), wp_ref[g], preferred_element_type=F32) * ps_ref[g]
        heads.append(yg.astype(BF16))
    mix = jnp.dot(jnp.concatenate(heads, axis=1), wo_ref[...], preferred_element_type=F32)
    x1 = x_ref[...] + g1_ref[...] * mix
    x1_ref[...] = x1
    y = x1 * lax.rsqrt(jnp.mean(x1 * x1, axis=-1, keepdims=True) + RMS_EPS) * n2_ref[...]
    h2 = y * (1.0 + sc2_ref[...]) + sh2_ref[...]
    for c in range(ROW_CHUNKS):
        h2_ref[pl.ds(c, tm, stride=ROW_CHUNKS), :] = h2[:, c * LANES:(c + 1) * LANES]

    n_rt = br_ref.shape[0]
    nt_dims = (((1,), (1,)), ((), ()))
    h_hi = h2.astype(BF16)
    h_lo = (h2 - h_hi.astype(F32)).astype(BF16)
    lt2 = lax.dot_general(wr_ref[...], h_hi, nt_dims, preferred_element_type=F32)
    lt = (lt2[:n_rt] + lt2[n_rt:] + lax.dot_general(wr_ref[0:n_rt, :], h_lo, nt_dims, preferred_element_type=F32)
          + br_ref[:, 0:1])
    gl = [lt[N_EXPERTS + i:N_EXPERTS + i + 1, :] for i in range(N_GROUPS)]
    gmax = jnp.maximum(jnp.maximum(gl[0], gl[1]), jnp.maximum(gl[2], gl[3]))
    gsel = jnp.where(gl[0] == gmax, 0, jnp.where(gl[1] == gmax, 1, jnp.where(gl[2] == gmax, 2, 3)))
    gden = sum(jnp.exp(g_ - gmax) for g_ in gl)
    e_in = lt[3 * EXPERTS_PER_GROUP:4 * EXPERTS_PER_GROUP, :]
    for i in (2, 1, 0):
        e_in = jnp.where(gsel == i, lt[i * EXPERTS_PER_GROUP:(i + 1) * EXPERTS_PER_GROUP, :], e_in)
    sub = lax.broadcasted_iota(I32, e_in.shape, 0)
    m1 = jnp.max(e_in, axis=0, keepdims=True)
    i1 = jnp.min(jnp.where(e_in == m1, sub, EXPERTS_PER_GROUP), axis=0, keepdims=True)
    rest = jnp.where(sub == i1, -jnp.inf, e_in)
    m2 = jnp.max(rest, axis=0, keepdims=True)
    i2 = jnp.min(jnp.where(rest == m2, sub, EXPERTS_PER_GROUP), axis=0, keepdims=True)
    p2 = jnp.exp(m2 - m1)
    w1 = 1.0 / ((1.0 + p2) * gden)
    w2 = p2 * w1
    row = lax.broadcasted_iota(I32, (SUBLANES, tm), 0)
    base = gsel * EXPERTS_PER_GROUP
    eid_ref[...] = jnp.where(row == 0, base + i1, jnp.where(row == 1, base + i2, 0))
    wts_ref[...] = jnp.where(row == 0, w1, jnp.where(row == 1, w2, 0.0))


def _mix(a_out, u, x2, mod4, w_pool_bf, pool_scale, w_out_bf, norm2_g, wr, br, bsz, seq, tm=512):
    n = x2.shape[0]
    spt = seq // tm
    mod_spec = lambda which: pl.BlockSpec((None, None, 1, D_MODEL), lambda b, j: (which, b, 0, 0))
    tok_spec = lambda width: pl.BlockSpec((tm, width), lambda b, j: (b * spt + j, 0))
    lane_spec = pl.BlockSpec((SUBLANES, tm), lambda b, j: (0, b * spt + j))
    return pl.pallas_call(
        functools.partial(_mix_kernel, seq=seq),
        grid=(bsz, spt),
        in_specs=[
            tok_spec(ATTN_WIDTH),
            pl.BlockSpec((seq, ATTN_WIDTH), lambda b, j: (b, 0)),
            tok_spec(D_MODEL),
            mod_spec(2), mod_spec(3), mod_spec(4),
            pl.BlockSpec((len(POOL_WINDOWS), POOL_GROUP_DIM, POOL_GROUP_DIM), lambda b, j: (0, 0, 0)),
            pl.BlockSpec((len(POOL_WINDOWS), 1, POOL_GROUP_DIM), lambda b, j: (0, 0, 0)),
            pl.BlockSpec((D_MODEL, D_MODEL), lambda b, j: (0, 0)),
            pl.BlockSpec((1, D_MODEL), lambda b, j: (0, 0)),
            pl.BlockSpec((2 * ROUTER_ROWS, D_MODEL), lambda b, j: (0, 0)),
            pl.BlockSpec((ROUTER_ROWS, LANES), lambda b, j: (0, 0)),
        ],
        out_specs=[
            tok_spec(D_MODEL),
            pl.BlockSpec((tm * ROW_CHUNKS, LANES), lambda b, j: (b * spt + j, 0)),
            lane_spec, lane_spec,
        ],
        out_shape=[
            jax.ShapeDtypeStruct((n, D_MODEL), F32),
            jax.ShapeDtypeStruct((n * ROW_CHUNKS, LANES), F32),
            jax.ShapeDtypeStruct((SUBLANES, n), I32),
            jax.ShapeDtypeStruct((SUBLANES, n), F32),
        ],
        scratch_shapes=[pltpu.VMEM((seq + 2 * POOL_HALO, ATTN_WIDTH), F32)],
        compiler_params=_cparams(("arbitrary", "arbitrary")),
    )(a_out, u, x2, mod4, mod4, mod4, w_pool_bf, pool_scale, w_out_bf, norm2_g, wr, br)


META_BLOCK_E, META_N_USED, META_PAD_END, META_COUNT = 0, 1, 2, 3


def _plan_kernel(eid_ref, dest_ref, meta_ref, cnt_ref, base_ref, *, n_blocks, meta_w, chunk):
    phase = pl.program_id(0)
    t = pl.program_id(1)
    nt = pl.num_programs(1)
    n_chunks = eid_ref.shape[1] // chunk
    sub = lax.broadcasted_iota(I32, (N_EXPERTS, chunk), 0)

    def one_hots(c):
        oh0 = sub == eid_ref[0:1, c * chunk:(c + 1) * chunk]
        oh1 = sub == eid_ref[1:2, c * chunk:(c + 1) * chunk]
        return oh0, oh1, jnp.where(oh0 | oh1, 1.0, 0.0)

    @pl.when((phase == 0) & (t == 0))
    def _():
        cnt_ref[...] = jnp.zeros_like(cnt_ref)

    @pl.when(phase == 0)
    def _():
        cnt_ref[...] += sum(jnp.sum(one_hots(c)[2], axis=1, keepdims=True) for c in range(n_chunks))

    @pl.when((phase == 0) & (t == nt - 1))
    def _():
        cnt = cnt_ref[...]
        padded = jnp.floor((cnt + (FFN_BLOCK - 1)) * (1.0 / FFN_BLOCK)) * FFN_BLOCK
        er = lax.broadcasted_iota(I32, (N_EXPERTS, N_EXPERTS), 0)
        ec = lax.broadcasted_iota(I32, (N_EXPERTS, N_EXPERTS), 1)
        tri = jnp.where(ec <= er, 1.0, 0.0)
        pad_end = jnp.dot(tri, padded, precision=HIGHEST, preferred_element_type=F32)
        base_ref[...] = pad_end - padded
        blk_row = (lax.broadcasted_iota(I32, (N_EXPERTS, meta_w), 1) * FFN_BLOCK).astype(F32)
        block_e = jnp.sum(jnp.where(pad_end[:, 0:1] <= blk_row, 1.0, 0.0), axis=0, keepdims=True)
        block_e = jnp.minimum(block_e, N_EXPERTS - 1.0)
        n_used = pad_end[N_EXPERTS - 1:N_EXPERTS, 0:1] * (1.0 / FFN_BLOCK)
        diag = (lax.broadcasted_iota(I32, (N_EXPERTS, LANES), 0)
                == lax.broadcasted_iota(I32, (N_EXPERTS, LANES), 1))
        pe_lane = jnp.sum(jnp.where(diag, pad_end, 0.0), axis=0, keepdims=True)
        cnt_lane = jnp.sum(jnp.where(diag, cnt, 0.0), axis=0, keepdims=True)
        zpad = jnp.zeros((1, meta_w - LANES), F32)
        row = lax.broadcasted_iota(I32, (SUBLANES, meta_w), 0)
        meta = jnp.where(row == META_BLOCK_E, block_e,
                         jnp.where(row == META_N_USED, n_used,
                                   jnp.where(row == META_PAD_END, jnp.concatenate([pe_lane, zpad], axis=1),
                                             jnp.where(row == META_COUNT, jnp.concatenate([cnt_lane, zpad], axis=1),
                                                       0.0))))
        meta_ref[...] = meta.astype(I32)

    @pl.when(phase == 1)
    def _():
        r = lax.broadcasted_iota(I32, (chunk, chunk), 0)
        c = lax.broadcasted_iota(I32, (chunk, chunk), 1)
        triu = jnp.where(r <= c, 1.0, 0.0).astype(BF16)
        row = lax.broadcasted_iota(I32, (SUBLANES, chunk), 0)
        base = base_ref[:, 0:1]
        for ci in range(n_chunks):
            oh0, oh1, oh = one_hots(ci)
            incl = jnp.dot(oh.astype(BF16), triu, preferred_element_type=F32)
            slot = base + incl - oh
            d0 = jnp.sum(jnp.where(oh0, slot, 0.0), axis=0, keepdims=True)
            d1 = jnp.sum(jnp.where(oh1, slot, 0.0), axis=0, keepdims=True)
            dest_ref[:, ci * chunk:(ci + 1) * chunk] = jnp.where(row == 0, d0, jnp.where(row == 1, d1, 0.0)).astype(I32)
            base = base + incl[:, chunk - 1:chunk]
        base_ref[...] = jnp.broadcast_to(base, base_ref.shape)


def _plan(eid, n_blocks, tt=2048, chunk=512):
    n = eid.shape[1]
    meta_w = pl.cdiv(n_blocks, LANES) * LANES
    return pl.pallas_call(
        functools.partial(_plan_kernel, n_blocks=n_blocks, meta_w=meta_w, chunk=chunk),
        grid=(2, n // tt),
        in_specs=[pl.BlockSpec((SUBLANES, tt), lambda p, t: (0, t))],
        out_specs=[
            pl.BlockSpec((SUBLANES, tt), lambda p, t: (0, p * t)),
            pl.BlockSpec((SUBLANES, meta_w), lambda p, t: (0, 0)),
        ],
        out_shape=[
            jax.ShapeDtypeStruct((SUBLANES, n), I32),
            jax.ShapeDtypeStruct((SUBLANES, meta_w), I32),
        ],
        scratch_shapes=[pltpu.VMEM((N_EXPERTS, LANES), F32), pltpu.VMEM((N_EXPERTS, LANES), F32)],
        compiler_params=_cparams(("arbitrary", "arbitrary")),
    )(eid)


def _dispatch_kernel(meta_ref, dest0_ref, dest1_ref, h2_ref, xs_hbm, zero_ref, zsem, sem, *, n_blocks):
    i = pl.program_id(0)
    tt = dest0_ref.shape[0]
    dest_refs = (dest0_ref, dest1_ref)
    blk_rows = FFN_BLOCK * ROW_CHUNKS

    def zero_copy(blk):
        start = pl.multiple_of(blk * blk_rows, blk_rows)
        return pltpu.make_async_copy(zero_ref, xs_hbm.at[pl.ds(start, blk_rows)], zsem)

    @pl.when(i == 0)
    def _():
        zero_ref[...] = jnp.zeros_like(zero_ref)
        for e in range(N_EXPERTS):
            @pl.when(meta_ref[META_COUNT, e] > 0)
            def _():
                blk = meta_ref[META_PAD_END, e] // FFN_BLOCK - 1
                cp = zero_copy(blk)
                cp.start()
                cp.wait()

        def tail(blk, carry):
            cp = zero_copy(blk)
            cp.start()
            cp.wait()
            return carry
        lax.fori_loop(meta_ref[META_N_USED, 0], n_blocks, tail, 0)

    def row_copy(t, slot):
        src = pl.multiple_of(t * ROW_CHUNKS, ROW_CHUNKS)
        dst = pl.multiple_of(slot * ROW_CHUNKS, ROW_CHUNKS)
        return pltpu.make_async_copy(h2_ref.at[pl.ds(src, ROW_CHUNKS)], xs_hbm.at[pl.ds(dst, ROW_CHUNKS)], sem)

    def issue(g, carry):
        for u in range(DMA_ISSUE_UNROLL):
            t = g * DMA_ISSUE_UNROLL + u
            for k in range(TOP_K):
                row_copy(t, dest_refs[k][t]).start(priority=k)
        return carry
    lax.fori_loop(0, tt // DMA_ISSUE_UNROLL, issue, 0)

    def drain(g, carry):
        for _ in range(DMA_DRAIN_UNROLL):
            row_copy(0, 0).wait()
        return carry
    lax.fori_loop(0, TOP_K * tt // DMA_DRAIN_UNROLL, drain, 0)


def _dispatch(meta, dests, h2rows, n_blocks, tt=1024):
    n = dests[0].shape[0]
    p_rows = n_blocks * FFN_BLOCK * ROW_CHUNKS
    idx_spec = pl.BlockSpec((tt,), lambda i, m: (i,), memory_space=pltpu.SMEM)
    return pl.pallas_call(
        functools.partial(_dispatch_kernel, n_blocks=n_blocks),
        grid_spec=pltpu.PrefetchScalarGridSpec(
            num_scalar_prefetch=1,
            grid=(n // tt,),
            in_specs=[
                idx_spec, idx_spec,
                pl.BlockSpec((tt * ROW_CHUNKS, LANES), lambda i, m: (i, 0)),
            ],
            out_specs=pl.BlockSpec(memory_space=pl.ANY),
            scratch_shapes=[
                pltpu.VMEM((FFN_BLOCK * ROW_CHUNKS, LANES), F32),
                pltpu.SemaphoreType.DMA,
                pltpu.SemaphoreType.DMA,
            ],
        ),
        out_shape=jax.ShapeDtypeStruct((p_rows, LANES), F32),
        compiler_params=_cparams(("arbitrary",)),
    )(meta, *dests, h2rows)


def _ffn_kernel(meta_ref, x_ref, wg_hbm, wu_hbm, wd_hbm, o_ref,
                wg_st_ref, wu_st_ref, wd_st_ref, wgu_bf_ref, wd_bf_ref, slot_ref, sems):
    n_used = meta_ref[META_N_USED, 0]
    blk_rows = FFN_BLOCK * ROW_CHUNKS

    def weight_copies(e, slot):
        return (pltpu.make_async_copy(wg_hbm.at[e], wg_st_ref.at[slot], sems.at[slot, 0]),
                pltpu.make_async_copy(wu_hbm.at[e], wu_st_ref.at[slot], sems.at[slot, 1]),
                pltpu.make_async_copy(wd_hbm.at[e], wd_st_ref.at[slot], sems.at[slot, 2]))

    @pl.when(pl.program_id(0) == 0)
    def _():
        slot_ref[0] = 0
        for cp in weight_copies(meta_ref[META_BLOCK_E, 0], 0):
            cp.start()

    for sub in range(FFN_BLOCKS_PER_STEP):
        _ffn_block(pl.program_id(0) * FFN_BLOCKS_PER_STEP + sub, sub * blk_rows, n_used, weight_copies,
                   meta_ref, x_ref, o_ref, wg_st_ref, wu_st_ref, wd_st_ref, wgu_bf_ref, wd_bf_ref, slot_ref)


def _ffn_block(i, row0, n_used, weight_copies, meta_ref, x_ref, o_ref,
               wg_st_ref, wu_st_ref, wd_st_ref, wgu_bf_ref, wd_bf_ref, slot_ref):
    used = i < n_used
    expert = meta_ref[META_BLOCK_E, i]
    prev_expert = meta_ref[META_BLOCK_E, jnp.maximum(i - 1, 0)]

    @pl.when(used & ((i == 0) | (expert != prev_expert)))
    def _():
        slot = slot_ref[0]
        for cp in weight_copies(expert, slot):
            cp.wait()
        wgu_bf_ref[:, 0:D_EXPERT] = wg_st_ref[slot].astype(BF16)
        wgu_bf_ref[:, D_EXPERT:] = wu_st_ref[slot].astype(BF16)
        wd_bf_ref[...] = wd_st_ref[slot].astype(BF16)
        next_first = meta_ref[META_PAD_END, expert] // FFN_BLOCK

        @pl.when(next_first < n_used)
        def _():
            for cp in weight_copies(meta_ref[META_BLOCK_E, next_first], 1 - slot):
                cp.start()
        slot_ref[0] = 1 - slot

    @pl.when(used)
    def _():
        x = jnp.concatenate([x_ref[pl.ds(row0 + c, FFN_BLOCK, stride=ROW_CHUNKS), :] for c in range(ROW_CHUNKS)],
                            axis=1).astype(BF16)
        gu = jnp.dot(x, wgu_bf_ref[...], preferred_element_type=F32)
        g, u = gu[:, 0:D_EXPERT], gu[:, D_EXPERT:]
        hid = (g * jax.nn.sigmoid(g) * u).astype(BF16)
        y = jnp.dot(hid, wd_bf_ref[...], preferred_element_type=F32)
        for c in range(ROW_CHUNKS):
            o_ref[pl.ds(row0 + c, FFN_BLOCK, stride=ROW_CHUNKS), :] = y[:, c * LANES:(c + 1) * LANES]

    @pl.when(jnp.logical_not(used))
    def _():
        o_ref[row0:row0 + FFN_BLOCK * ROW_CHUNKS, :] = jnp.zeros((FFN_BLOCK * ROW_CHUNKS, LANES), F32)


def _ffn(meta, xs, w_gate, w_up, w_down, n_blocks):
    assert n_blocks % FFN_BLOCKS_PER_STEP == 0
    blk_rows = FFN_BLOCK * ROW_CHUNKS
    row_spec = pl.BlockSpec((FFN_BLOCKS_PER_STEP * blk_rows, LANES), lambda i, m: (i, 0))
    return pl.pallas_call(
        _ffn_kernel,
        grid_spec=pltpu.PrefetchScalarGridSpec(
            num_scalar_prefetch=1,
            grid=(n_blocks // FFN_BLOCKS_PER_STEP,),
            in_specs=[
                row_spec,
                pl.BlockSpec(memory_space=pl.ANY),
                pl.BlockSpec(memory_space=pl.ANY),
                pl.BlockSpec(memory_space=pl.ANY),
            ],
            out_specs=row_spec,
            scratch_shapes=[
                pltpu.VMEM((2, D_MODEL, D_EXPERT), F32),
                pltpu.VMEM((2, D_MODEL, D_EXPERT), F32),
                pltpu.VMEM((2, D_EXPERT, D_MODEL), F32),
                pltpu.VMEM((D_MODEL, 2 * D_EXPERT), BF16),
                pltpu.VMEM((D_EXPERT, D_MODEL), BF16),
                pltpu.SMEM((1,), I32),
                pltpu.SemaphoreType.DMA((2, 3)),
            ],
        ),
        out_shape=jax.ShapeDtypeStruct(xs.shape, F32),
        compiler_params=_cparams(("arbitrary",)),
    )(meta, xs, w_gate, w_up, w_down)


def _combine_kernel(dest0_ref, dest1_ref, next_dest0_ref, next_dest1_ref, wts_ref, x1_ref, g2_ref, fg_ref, yb_hbm,
                    o_ref, buf_ref, sems):
    i = pl.program_id(0)
    last = pl.num_programs(0) - 1
    tt = x1_ref.shape[0]
    slot = i % 2

    def row_copy(buf_slot, k, t, row):
        src = pl.multiple_of(row * ROW_CHUNKS, ROW_CHUNKS)
        dst = pl.multiple_of(t * ROW_CHUNKS, ROW_CHUNKS)
        return pltpu.make_async_copy(yb_hbm.at[pl.ds(src, ROW_CHUNKS)],
                                     buf_ref.at[buf_slot, k, pl.ds(dst, ROW_CHUNKS)], sems.at[buf_slot])

    def issue_tile(idx_refs, buf_slot):
        def body(g, carry):
            for u in range(DMA_ISSUE_UNROLL):
                t = g * DMA_ISSUE_UNROLL + u
                for k in range(TOP_K):
                    row_copy(buf_slot, k, t, idx_refs[k][t]).start(priority=k)
            return carry
        lax.fori_loop(0, tt // DMA_ISSUE_UNROLL, body, 0)

    @pl.when(i == 0)
    def _():
        issue_tile((dest0_ref, dest1_ref), 0)

    @pl.when(i < last)
    def _():
        issue_tile((next_dest0_ref, next_dest1_ref), 1 - slot)

    def drain(g, carry):
        for _ in range(DMA_DRAIN_UNROLL):
            row_copy(slot, 0, 0, 0).wait()
        return carry
    lax.fori_loop(0, TOP_K * tt // DMA_DRAIN_UNROLL, drain, 0)

    eye = jnp.where(lax.broadcasted_iota(I32, (SUBLANES, LANES), 0) == lax.broadcasted_iota(I32, (SUBLANES, LANES), 1),
                    1.0, 0.0)
    wcol = lax.dot_general(wts_ref[...], eye, (((0,), (0,)), ((), ())), precision=HIGHEST,
                           preferred_element_type=F32)
    ya = jnp.concatenate([buf_ref[slot, 0, pl.ds(c, tt, stride=ROW_CHUNKS), :] for c in range(ROW_CHUNKS)], axis=1)
    yb = jnp.concatenate([buf_ref[slot, 1, pl.ds(c, tt, stride=ROW_CHUNKS), :] for c in range(ROW_CHUNKS)], axis=1)
    y = ya * wcol[:, 0:1] + yb * wcol[:, 1:2]
    x2 = x1_ref[...] + g2_ref[...] * y
    o_ref[...] = x2 * lax.rsqrt(jnp.mean(x2 * x2, axis=-1, keepdims=True) + RMS_EPS) * fg_ref[...]


def _combine(dests, wts, x1, mod4, final_g, yb, seq, tt=512):
    n = x1.shape[0]
    spt = seq // tt
    steps = n // tt
    idx_spec = pl.BlockSpec((tt,), lambda i: (i,), memory_space=pltpu.SMEM)
    next_idx_spec = pl.BlockSpec((tt,), lambda i: (jnp.minimum(i + 1, steps - 1),), memory_space=pltpu.SMEM)
    return pl.pallas_call(
        _combine_kernel,
        grid=(steps,),
        in_specs=[
            idx_spec, idx_spec, next_idx_spec, next_idx_spec,
            pl.BlockSpec((SUBLANES, tt), lambda i: (0, i)),
            pl.BlockSpec((tt, D_MODEL), lambda i: (i, 0)),
            pl.BlockSpec((None, None, 1, D_MODEL), lambda i: (5, i // spt, 0, 0)),
            pl.BlockSpec((1, D_MODEL), lambda i: (0, 0)),
            pl.BlockSpec(memory_space=pl.ANY),
        ],
        out_specs=pl.BlockSpec((tt, D_MODEL), lambda i: (i, 0)),
        out_shape=jax.ShapeDtypeStruct((n, D_MODEL), F32),
        scratch_shapes=[pltpu.VMEM((2, 2, tt * ROW_CHUNKS, LANES), F32), pltpu.SemaphoreType.DMA((2,))],
        compiler_params=_cparams(("arbitrary",)),
    )(*dests, *dests, wts, x1, mod4, final_g, yb)


def kernel(x, c, w_ada, b_ada, norm1_g, w_in, lambda_q1, lambda_k1, lambda_q2, lambda_k2, subln_g, w_pool,
           pool_scale, w_out, norm2_g, w_router_group, b_router_group, w_router_expert, b_router_expert,
           w_gate, w_up, w_down, final_g):
    bsz, seq, d = x.shape
    n = bsz * seq
    x2 = x.reshape(n, d)

    mod = _ada(c, w_ada[0], b_ada[0])
    mod4 = mod.reshape(mod.shape[0], bsz, 1, d)

    q, k, v, u = _inproj(x2, mod4, norm1_g, w_in[0].astype(BF16), seq)
    a_out = _attention(q, k, v, lambda_q1, lambda_k1, lambda_q2, lambda_k2, subln_g, bsz, seq)

    pad_rows = ROUTER_ROWS - N_EXPERTS - N_GROUPS
    wr = jnp.concatenate([w_router_expert[0].T, w_router_group[0].T, jnp.zeros((pad_rows, d), F32)], axis=0)
    wr_hi = wr.astype(BF16)
    wr_lo = (wr - wr_hi.astype(F32)).astype(BF16)
    wr = jnp.concatenate([wr_hi, wr_lo], axis=0)
    br = jnp.concatenate([b_router_expert[0], b_router_group[0], jnp.zeros((pad_rows,), F32)])
    br = jnp.broadcast_to(br[:, None], (br.shape[0], LANES))
    x1, h2rows, eid, wts = _mix(a_out, u, x2, mod4, w_pool[0].astype(BF16),
                                pool_scale[0].reshape(len(POOL_WINDOWS), 1, POOL_GROUP_DIM),
                                w_out[0].astype(BF16), norm2_g, wr, br, bsz, seq)

    n_blocks = pl.cdiv(2 * n, FFN_BLOCK) + N_EXPERTS
    dest, meta = _plan(eid, n_blocks)
    dests = tuple(dest[k] for k in range(TOP_K))
    xs = _dispatch(meta, dests, h2rows, n_blocks)
    yb = _ffn(meta, xs, w_gate[0], w_up[0], w_down[0], n_blocks)
    out = _combine(dests, wts, x1, mod4, final_g.reshape(1, d), yb, seq)
    return out.reshape(bsz, seq, d)
```

```python
import functools
import math

import jax
import jax.numpy as jnp
from jax import lax
from jax.experimental import pallas as pl
from jax.experimental.pallas import tpu as pltpu

F32 = jnp.float32
BF16 = jnp.bfloat16
I32 = jnp.int32
HIGHEST = lax.Precision.HIGHEST

D_MODEL = 1024
ATTN_WIDTH = 512
N_HEADS = 4
HEAD_DIM = 64
VALUE_DIM = 128
ROT_DIM = 16
ROPE_THETA = 500000.0
POOL_WINDOWS = (2, 4, 8, 16)
POOL_GROUP_DIM = 128
N_GROUPS = 4
EXPERTS_PER_GROUP = 8
N_EXPERTS = 32
TOP_K = 2
D_EXPERT = 512
RMS_EPS = 1e-6
LAMBDA_INIT = 0.8 - 0.6 * math.exp(-0.3 * 0)

LANES = 128
SUBLANES = 8
ROW_CHUNKS = D_MODEL // LANES

POOL_HALO = 8
FFN_BLOCK = 256
FFN_BLOCKS_PER_STEP = 2
DMA_ISSUE_UNROLL = 8
DMA_DRAIN_UNROLL = 16
ROUTER_ROWS = 48
VMEM_LIMIT = 56 * 1024 * 1024


def _cparams(sem, vmem=VMEM_LIMIT):
    return pltpu.CompilerParams(dimension_semantics=sem, vmem_limit_bytes=vmem)


def _ada_kernel(c_ref, w_ref, b_ref, o_ref):
    c = c_ref[...]
    act = c * jax.nn.sigmoid(c)
    o_ref[...] = jnp.dot(act, w_ref[...], precision=HIGHEST, preferred_element_type=F32) + b_ref[...]


def _ada(c, w_ada, b_ada):
    bsz = c.shape[0]
    n_mod = w_ada.shape[1] // D_MODEL
    return pl.pallas_call(
        _ada_kernel,
        grid=(n_mod,),
        in_specs=[
            pl.BlockSpec((bsz, D_MODEL), lambda j: (0, 0)),
            pl.BlockSpec((D_MODEL, D_MODEL), lambda j: (0, j)),
            pl.BlockSpec((1, D_MODEL), lambda j: (0, j)),
        ],
        out_specs=pl.BlockSpec((None, bsz, D_MODEL), lambda j: (j, 0, 0)),
        out_shape=jax.ShapeDtypeStruct((n_mod, bsz, D_MODEL), F32),
        compiler_params=_cparams(("arbitrary",)),
    )(c, w_ada, b_ada.reshape(1, -1))


def _inproj_kernel(x_ref, sh_ref, sc_ref, g_ref, w_ref, cos_ref, sa_ref, sb_ref,
                   q_ref, k_ref, v_ref, u_ref):
    x = x_ref[...]
    y = x * lax.rsqrt(jnp.mean(x * x, axis=-1, keepdims=True) + RMS_EPS) * g_ref[...]
    h = (y * (1.0 + sc_ref[...]) + sh_ref[...]).astype(BF16)
    cos, sa, sb = cos_ref[...], sa_ref[...], sb_ref[...]
    for part, o_ref, scale in ((0, q_ref, HEAD_DIM ** -0.5 * math.log2(math.e)), (1, k_ref, None)):
        z = jnp.dot(h, w_ref[:, part * ATTN_WIDTH:(part + 1) * ATTN_WIDTH], preferred_element_type=F32)
        for hd in range(N_HEADS):
            zc = z[:, hd * LANES:(hd + 1) * LANES]
            r = zc * cos + pltpu.roll(zc, ROT_DIM // 2, 1) * sa + pltpu.roll(zc, LANES - ROT_DIM // 2, 1) * sb
            if scale is not None:
                r = r * scale
            o_ref[:, hd * LANES:(hd + 1) * LANES] = r.astype(BF16)
    v_ref[...] = jnp.dot(h, w_ref[:, 2 * ATTN_WIDTH:3 * ATTN_WIDTH], preferred_element_type=F32).astype(BF16)
    u_ref[...] = jnp.dot(h, w_ref[:, 3 * ATTN_WIDTH:], preferred_element_type=F32).astype(BF16)


def _rope_lane_tables(seq):
    pos = jnp.arange(seq, dtype=F32)
    inv_freq = ROPE_THETA ** (-jnp.arange(0, ROT_DIM, 2, dtype=F32) / ROT_DIM)
    ang = pos[:, None] * inv_freq[None, :]
    cos, sin = jnp.cos(ang), jnp.sin(ang)
    half = ROT_DIM // 2
    ones = jnp.ones((seq, HEAD_DIM - ROT_DIM), F32)
    zeros_h = jnp.zeros((seq, half), F32)
    zeros_r = jnp.zeros((seq, HEAD_DIM - ROT_DIM), F32)
    cos_t = jnp.concatenate([cos, cos, ones], axis=1)
    sa_t = jnp.concatenate([zeros_h, sin, zeros_r], axis=1)
    sb_t = jnp.concatenate([-sin, zeros_h, zeros_r], axis=1)
    rep = LANES // HEAD_DIM
    return jnp.tile(cos_t, (1, rep)), jnp.tile(sa_t, (1, rep)), jnp.tile(sb_t, (1, rep))


def _inproj(x2, mod4, norm1_g, w_in_bf, seq, tm=512):
    n = x2.shape[0]
    spt = seq // tm
    cos_t, sa_t, sb_t = _rope_lane_tables(seq)
    mod_spec = lambda which: pl.BlockSpec((None, None, 1, D_MODEL), lambda i: (which, i // spt, 0, 0))
    tab_spec = pl.BlockSpec((tm, LANES), lambda i: (i % spt, 0))
    out_spec = pl.BlockSpec((tm, ATTN_WIDTH), lambda i: (i, 0))
    out_sds = jax.ShapeDtypeStruct((n, ATTN_WIDTH), BF16)
    return pl.pallas_call(
        _inproj_kernel,
        grid=(n // tm,),
        in_specs=[
            pl.BlockSpec((tm, D_MODEL), lambda i: (i, 0)),
            mod_spec(0), mod_spec(1),
            pl.BlockSpec((1, D_MODEL), lambda i: (0, 0)),
            pl.BlockSpec((D_MODEL, 4 * ATTN_WIDTH), lambda i: (0, 0)),
            tab_spec, tab_spec, tab_spec,
        ],
        out_specs=[out_spec] * 4,
        out_shape=[out_sds] * 4,
        compiler_params=_cparams(("arbitrary",)),
    )(x2, mod4, mod4, norm1_g, w_in_bf, cos_t, sa_t, sb_t)


def _attn_kernel(lq1_ref, lk1_ref, lq2_ref, lk2_ref, sg_ref, q_ref, k_ref, v_ref, o_ref,
                 sa_ref, sb_ref, ma_ref, mb_ref, *, tq):
    lam = (jnp.exp(jnp.sum(lq1_ref[...] * lk1_ref[...], axis=-1, keepdims=True))
           - jnp.exp(jnp.sum(lq2_ref[...] * lk2_ref[...], axis=-1, keepdims=True)) + LAMBDA_INIT)
    n_tiles = q_ref.shape[0] // tq
    bufs = ((sa_ref, ma_ref), (sb_ref, mb_ref))

    def score_stage(j, s_ref, m_ref):
        q = q_ref[j * tq:(j + 1) * tq, :]
        lane = lax.broadcasted_iota(I32, q.shape, 1)
        zero = jnp.zeros_like(q)
        qq = jnp.concatenate([jnp.where(lane < HEAD_DIM, q, zero), jnp.where(lane >= HEAD_DIM, q, zero)], axis=0)
        s = lax.dot_general(qq, k_ref[...], (((1,), (1,)), ((), ())), preferred_element_type=F32)
        s_ref[...] = s
        m_ref[...] = jnp.max(s, axis=-1, keepdims=True)

    def value_stage(j, s_ref, m_ref):
        p = jnp.exp2(s_ref[...] - m_ref[...])
        l = jnp.sum(p, axis=-1, keepdims=True)
        pb = p.astype(BF16)
        ratio = (lam * l[:tq] / l[tq:]).astype(BF16)
        o = jnp.dot(pb[:tq] - pb[tq:] * ratio, v_ref[...], preferred_element_type=F32) / l[:tq]
        o = o * lax.rsqrt(jnp.mean(o * o, axis=-1, keepdims=True) + RMS_EPS) * sg_ref[...]
        o_ref[j * tq:(j + 1) * tq, :] = (o * (1.0 - LAMBDA_INIT)).astype(BF16)

    score_stage(0, *bufs[0])
    for j in range(n_tiles):
        if j + 1 < n_tiles:
            score_stage(j + 1, *bufs[(j + 1) % 2])
        value_stage(j, *bufs[j % 2])


def _attention(q, k, v, lq1, lk1, lq2, lk2, subln_g, bsz, seq, tq=256):
    n = q.shape[0]
    vec_spec = lambda width: pl.BlockSpec((1, width), lambda b, h: (0, 0))
    head_spec = pl.BlockSpec((seq, LANES), lambda b, h: (b, h))
    return pl.pallas_call(
        functools.partial(_attn_kernel, tq=tq),
        grid=(bsz, N_HEADS),
        in_specs=[vec_spec(HEAD_DIM)] * 4 + [vec_spec(VALUE_DIM), head_spec, head_spec, head_spec],
        out_specs=head_spec,
        out_shape=jax.ShapeDtypeStruct((n, ATTN_WIDTH), BF16),
        scratch_shapes=[pltpu.VMEM((2 * tq, seq), F32), pltpu.VMEM((2 * tq, seq), F32),
                        pltpu.VMEM((2 * tq, 1), F32), pltpu.VMEM((2 * tq, 1), F32)],
        compiler_params=_cparams(("arbitrary",) * 2),
    )(lq1, lk1, lq2, lk2, subln_g, q, k, v)


def _mix_kernel(a_ref, u_ref, x_ref, g1_ref, sh2_ref, sc2_ref, wp_ref, ps_ref, wo_ref, n2_ref,
                wr_ref, br_ref, x1_ref, h2_ref, eid_ref, wts_ref, up_ref, *, seq):
    j = pl.program_id(1)
    tm = x_ref.shape[0]
    pool_w = u_ref.shape[1]

    @pl.when(j == 0)
    def _():
        up_ref[0:POOL_HALO, :] = jnp.zeros((POOL_HALO, pool_w), F32)
        up_ref[POOL_HALO + seq:, :] = jnp.zeros((POOL_HALO, pool_w), F32)
        up_ref[POOL_HALO:POOL_HALO + seq, :] = u_ref[...].astype(F32)

    r0 = pl.multiple_of(j * tm, tm)
    pos = r0 + lax.broadcasted_iota(I32, (tm, 1), 0)
    heads = [a_ref[...]]
    for g, win in enumerate(POOL_WINDOWS):
        left = win // 2
        right = win - 1 - left
        cols = slice(g * POOL_GROUP_DIM, (g + 1) * POOL_GROUP_DIM)
        win_rows = up_ref[pl.ds(r0, tm + 2 * POOL_HALO), cols]
        tot, span = win_rows, 1
        while span < win:
            tot = tot[:tot.shape[0] - span] + tot[span:]
            span *= 2
        tot = tot[POOL_HALO - left:POOL_HALO - left + tm]
        cnt = (jnp.minimum(pos + (right + 1), seq) - jnp.maximum(pos - left, 0)).astype(F32)
        d = tot / cnt - win_rows[POOL_HALO:POOL_HALO + tm]
        yg = jnp.dot(d.astype(BF16), wp_ref[g], preferred_element_type=F32) * ps_ref[g]
        heads.append(yg.astype(BF16))
    mix = jnp.dot(jnp.concatenate(heads, axis=1), wo_ref[...], preferred_element_type=F32)
    x1 = x_ref[...] + g1_ref[...] * mix
    x1_ref[...] = x1
    y = x1 * lax.rsqrt(jnp.mean(x1 * x1, axis=-1, keepdims=True) + RMS_EPS) * n2_ref[...]
    h2 = y * (1.0 + sc2_ref[...]) + sh2_ref[...]
    for c in range(ROW_CHUNKS):
        h2_ref[pl.ds(c, tm, stride=ROW_CHUNKS), :] = h2[:, c * LANES:(c + 1) * LANES]

    n_rt = br_ref.shape[0]
    nt_dims = (((1,), (1,)), ((), ()))
    h_hi = h2.astype(BF16)
    h_lo = (h2 - h_hi.astype(F32)).astype(BF16)
    lt2 = lax.dot_general(wr_ref[...], h_hi, nt_dims, preferred_element_type=F32)
    lt = (lt2[:n_rt] + lt2[n_rt:] + lax.dot_general(wr_ref[0:n_rt, :], h_lo, nt_dims, preferred_element_type=F32)
          + br_ref[:, 0:1])
    gl = [lt[N_EXPERTS + i:N_EXPERTS + i + 1, :] for i in range(N_GROUPS)]
    gmax = jnp.maximum(jnp.maximum(gl[0], gl[1]), jnp.maximum(gl[2], gl[3]))
    gsel = jnp.where(gl[0] == gmax, 0, jnp.where(gl[1] == gmax, 1, jnp.where(gl[2] == gmax, 2, 3)))
    gden = sum(jnp.exp(g_ - gmax) for g_ in gl)
    e_in = lt[3 * EXPERTS_PER_GROUP:4 * EXPERTS_PER_GROUP, :]
    for i in (2, 1, 0):
        e_in = jnp.where(gsel == i, lt[i * EXPERTS_PER_GROUP:(i + 1) * EXPERTS_PER_GROUP, :], e_in)
    sub = lax.broadcasted_iota(I32, e_in.shape, 0)
    m1 = jnp.max(e_in, axis=0, keepdims=True)
    i1 = jnp.min(jnp.where(e_in == m1, sub, EXPERTS_PER_GROUP), axis=0, keepdims=True)
    rest = jnp.where(sub == i1, -jnp.inf, e_in)
    m2 = jnp.max(rest, axis=0, keepdims=True)
    i2 = jnp.min(jnp.where(rest == m2, sub, EXPERTS_PER_GROUP), axis=0, keepdims=True)
    p2 = jnp.exp(m2 - m1)
    w1 = 1.0 / ((1.0 + p2) * gden)
    w2 = p2 * w1
    row = lax.broadcasted_iota(I32, (SUBLANES, tm), 0)
    base = gsel * EXPERTS_PER_GROUP
    eid_ref[...] = jnp.where(row == 0, base + i1, jnp.where(row == 1, base + i2, 0))
    wts_ref[...] = jnp.where(row == 0, w1, jnp.where(row == 1, w2, 0.0))


def _mix(a_out, u, x2, mod4, w_pool_bf, pool_scale, w_out_bf, norm2_g, wr, br, bsz, seq, tm=512):
    n = x2.shape[0]
    spt = seq // tm
    mod_spec = lambda which: pl.BlockSpec((None, None, 1, D_MODEL), lambda b, j: (which, b, 0, 0))
    tok_spec = lambda width: pl.BlockSpec((tm, width), lambda b, j: (b * spt + j, 0))
    lane_spec = pl.BlockSpec((SUBLANES, tm), lambda b, j: (0, b * spt + j))
    return pl.pallas_call(
        functools.partial(_mix_kernel, seq=seq),
        grid=(bsz, spt),
        in_specs=[
            tok_spec(ATTN_WIDTH),
            pl.BlockSpec((seq, ATTN_WIDTH), lambda b, j: (b, 0)),
            tok_spec(D_MODEL),
            mod_spec(2), mod_spec(3), mod_spec(4),
            pl.BlockSpec((len(POOL_WINDOWS), POOL_GROUP_DIM, POOL_GROUP_DIM), lambda b, j: (0, 0, 0)),
            pl.BlockSpec((len(POOL_WINDOWS), 1, POOL_GROUP_DIM), lambda b, j: (0, 0, 0)),
            pl.BlockSpec((D_MODEL, D_MODEL), lambda b, j: (0, 0)),
            pl.BlockSpec((1, D_MODEL), lambda b, j: (0, 0)),
            pl.BlockSpec((2 * ROUTER_ROWS, D_MODEL), lambda b, j: (0, 0)),
            pl.BlockSpec((ROUTER_ROWS, LANES), lambda b, j: (0, 0)),
        ],
        out_specs=[
            tok_spec(D_MODEL),
            pl.BlockSpec((tm * ROW_CHUNKS, LANES), lambda b, j: (b * spt + j, 0)),
            lane_spec, lane_spec,
        ],
        out_shape=[
            jax.ShapeDtypeStruct((n, D_MODEL), F32),
            jax.ShapeDtypeStruct((n * ROW_CHUNKS, LANES), F32),
            jax.ShapeDtypeStruct((SUBLANES, n), I32),
            jax.ShapeDtypeStruct((SUBLANES, n), F32),
        ],
        scratch_shapes=[pltpu.VMEM((seq + 2 * POOL_HALO, ATTN_WIDTH), F32)],
        compiler_params=_cparams(("arbitrary", "arbitrary")),
    )(a_out, u, x2, mod4, mod4, mod4, w_pool_bf, pool_scale, w_out_bf, norm2_g, wr, br)


META_BLOCK_E, META_N_USED, META_PAD_END, META_COUNT = 0, 1, 2, 3


def _plan_kernel(eid_ref, dest_ref, meta_ref, cnt_ref, base_ref, *, n_blocks, meta_w, chunk):
    phase = pl.program_id(0)
    t = pl.program_id(1)
    nt = pl.num_programs(1)
    n_chunks = eid_ref.shape[1] // chunk
    sub = lax.broadcasted_iota(I32, (N_EXPERTS, chunk), 0)

    def one_hots(c):
        oh0 = sub == eid_ref[0:1, c * chunk:(c + 1) * chunk]
        oh1 = sub == eid_ref[1:2, c * chunk:(c + 1) * chunk]
        return oh0, oh1, jnp.where(oh0 | oh1, 1.0, 0.0)

    @pl.when((phase == 0) & (t == 0))
    def _():
        cnt_ref[...] = jnp.zeros_like(cnt_ref)

    @pl.when(phase == 0)
    def _():
        cnt_ref[...] += sum(jnp.sum(one_hots(c)[2], axis=1, keepdims=True) for c in range(n_chunks))

    @pl.when((phase == 0) & (t == nt - 1))
    def _():
        cnt = cnt_ref[...]
        padded = jnp.floor((cnt + (FFN_BLOCK - 1)) * (1.0 / FFN_BLOCK)) * FFN_BLOCK
        er = lax.broadcasted_iota(I32, (N_EXPERTS, N_EXPERTS), 0)
        ec = lax.broadcasted_iota(I32, (N_EXPERTS, N_EXPERTS), 1)
        tri = jnp.where(ec <= er, 1.0, 0.0)
        pad_end = jnp.dot(tri, padded, precision=HIGHEST, preferred_element_type=F32)
        base_ref[...] = pad_end - padded
        blk_row = (lax.broadcasted_iota(I32, (N_EXPERTS, meta_w), 1) * FFN_BLOCK).astype(F32)
        block_e = jnp.sum(jnp.where(pad_end[:, 0:1] <= blk_row, 1.0, 0.0), axis=0, keepdims=True)
        block_e = jnp.minimum(block_e, N_EXPERTS - 1.0)
        n_used = pad_end[N_EXPERTS - 1:N_EXPERTS, 0:1] * (1.0 / FFN_BLOCK)
        diag = (lax.broadcasted_iota(I32, (N_EXPERTS, LANES), 0)
                == lax.broadcasted_iota(I32, (N_EXPERTS, LANES), 1))
        pe_lane = jnp.sum(jnp.where(diag, pad_end, 0.0), axis=0, keepdims=True)
        cnt_lane = jnp.sum(jnp.where(diag, cnt, 0.0), axis=0, keepdims=True)
        zpad = jnp.zeros((1, meta_w - LANES), F32)
        row = lax.broadcasted_iota(I32, (SUBLANES, meta_w), 0)
        meta = jnp.where(row == META_BLOCK_E, block_e,
                         jnp.where(row == META_N_USED, n_used,
                                   jnp.where(row == META_PAD_END, jnp.concatenate([pe_lane, zpad], axis=1),
                                             jnp.where(row == META_COUNT, jnp.concatenate([cnt_lane, zpad], axis=1),
                                                       0.0))))
        meta_ref[...] = meta.astype(I32)

    @pl.when(phase == 1)
    def _():
        r = lax.broadcasted_iota(I32, (chunk, chunk), 0)
        c = lax.broadcasted_iota(I32, (chunk, chunk), 1)
        triu = jnp.where(r <= c, 1.0, 0.0).astype(BF16)
        row = lax.broadcasted_iota(I32, (SUBLANES, chunk), 0)
        base = base_ref[:, 0:1]
        for ci in range(n_chunks):
            oh0, oh1, oh = one_hots(ci)
            incl = jnp.dot(oh.astype(BF16), triu, preferred_element_type=F32)
            slot = base + incl - oh
            d0 = jnp.sum(jnp.where(oh0, slot, 0.0), axis=0, keepdims=True)
            d1 = jnp.sum(jnp.where(oh1, slot, 0.0), axis=0, keepdims=True)
            dest_ref[:, ci * chunk:(ci + 1) * chunk] = jnp.where(row == 0, d0, jnp.where(row == 1, d1, 0.0)).astype(I32)
            base = base + incl[:, chunk - 1:chunk]
        base_ref[...] = jnp.broadcast_to(base, base_ref.shape)


def _plan(eid, n_blocks, tt=2048, chunk=512):
    n = eid.shape[1]
    meta_w = pl.cdiv(n_blocks, LANES) * LANES
    return pl.pallas_call(
        functools.partial(_plan_kernel, n_blocks=n_blocks, meta_w=meta_w, chunk=chunk),
        grid=(2, n // tt),
        in_specs=[pl.BlockSpec((SUBLANES, tt), lambda p, t: (0, t))],
        out_specs=[
            pl.BlockSpec((SUBLANES, tt), lambda p, t: (0, p * t)),
            pl.BlockSpec((SUBLANES, meta_w), lambda p, t: (0, 0)),
        ],
        out_shape=[
            jax.ShapeDtypeStruct((SUBLANES, n), I32),
            jax.ShapeDtypeStruct((SUBLANES, meta_w), I32),
        ],
        scratch_shapes=[pltpu.VMEM((N_EXPERTS, LANES), F32), pltpu.VMEM((N_EXPERTS, LANES), F32)],
        compiler_params=_cparams(("arbitrary", "arbitrary")),
    )(eid)


def _dispatch_kernel(meta_ref, dest0_ref, dest1_ref, h2_ref, xs_hbm, zero_ref, zsem, sem, *, n_blocks):
    i = pl.program_id(0)
    tt = dest0_ref.shape[0]
    dest_refs = (dest0_ref, dest1_ref)
    blk_rows = FFN_BLOCK * ROW_CHUNKS

    def zero_copy(blk):
        start = pl.multiple_of(blk * blk_rows, blk_rows)
        return pltpu.make_async_copy(zero_ref, xs_hbm.at[pl.ds(start, blk_rows)], zsem)

    @pl.when(i == 0)
    def _():
        zero_ref[...] = jnp.zeros_like(zero_ref)
        for e in range(N_EXPERTS):
            @pl.when(meta_ref[META_COUNT, e] > 0)
            def _():
                blk = meta_ref[META_PAD_END, e] // FFN_BLOCK - 1
                cp = zero_copy(blk)
                cp.start()
                cp.wait()

        def tail(blk, carry):
            cp = zero_copy(blk)
            cp.start()
            cp.wait()
            return carry
        lax.fori_loop(meta_ref[META_N_USED, 0], n_blocks, tail, 0)

    def row_copy(t, slot):
        src = pl.multiple_of(t * ROW_CHUNKS, ROW_CHUNKS)
        dst = pl.multiple_of(slot * ROW_CHUNKS, ROW_CHUNKS)
        return pltpu.make_async_copy(h2_ref.at[pl.ds(src, ROW_CHUNKS)], xs_hbm.at[pl.ds(dst, ROW_CHUNKS)], sem)

    def issue(g, carry):
        for u in range(DMA_ISSUE_UNROLL):
            t = g * DMA_ISSUE_UNROLL + u
            for k in range(TOP_K):
                row_copy(t, dest_refs[k][t]).start(priority=k)
        return carry
    lax.fori_loop(0, tt // DMA_ISSUE_UNROLL, issue, 0)

    def drain(g, carry):
        for _ in range(DMA_DRAIN_UNROLL):
            row_copy(0, 0).wait()
        return carry
    lax.fori_loop(0, TOP_K * tt // DMA_DRAIN_UNROLL, drain, 0)


def _dispatch(meta, dests, h2rows, n_blocks, tt=1024):
    n = dests[0].shape[0]
    p_rows = n_blocks * FFN_BLOCK * ROW_CHUNKS
    idx_spec = pl.BlockSpec((tt,), lambda i, m: (i,), memory_space=pltpu.SMEM)
    return pl.pallas_call(
        functools.partial(_dispatch_kernel, n_blocks=n_blocks),
        grid_spec=pltpu.PrefetchScalarGridSpec(
            num_scalar_prefetch=1,
            grid=(n // tt,),
            in_specs=[
                idx_spec, idx_spec,
                pl.BlockSpec((tt * ROW_CHUNKS, LANES), lambda i, m: (i, 0)),
            ],
            out_specs=pl.BlockSpec(memory_space=pl.ANY),
            scratch_shapes=[
                pltpu.VMEM((FFN_BLOCK * ROW_CHUNKS, LANES), F32),
                pltpu.SemaphoreType.DMA,
                pltpu.SemaphoreType.DMA,
            ],
        ),
        out_shape=jax.ShapeDtypeStruct((p_rows, LANES), F32),
        compiler_params=_cparams(("arbitrary",)),
    )(meta, *dests, h2rows)


def _ffn_kernel(meta_ref, x_ref, wg_hbm, wu_hbm, wd_hbm, o_ref,
                wg_st_ref, wu_st_ref, wd_st_ref, wgu_bf_ref, wd_bf_ref, parity_ref, sems):
    n_used = meta_ref[META_N_USED, 0]
    blk_rows = FFN_BLOCK * ROW_CHUNKS

    def weight_copies(e, slot):
        return (pltpu.make_async_copy(wg_hbm.at[e], wg_st_ref.at[slot], sems.at[slot, 0]),
                pltpu.make_async_copy(wu_hbm.at[e], wu_st_ref.at[slot], sems.at[slot, 1]),
                pltpu.make_async_copy(wd_hbm.at[e], wd_st_ref.at[slot], sems.at[slot, 2]))

    @pl.when(pl.program_id(0) == 0)
    def _():
        parity_ref[0] = 1
        for cp in weight_copies(meta_ref[META_BLOCK_E, 0], 0):
            cp.start()

    buffers = []
    for sub in range(FFN_BLOCKS_PER_STEP):
        i = pl.program_id(0) * FFN_BLOCKS_PER_STEP + sub
        expert = meta_ref[META_BLOCK_E, i]
        prev_expert = meta_ref[META_BLOCK_E, jnp.maximum(i - 1, 0)]

        @pl.when((i < n_used) & ((i == 0) | (expert != prev_expert)))
        def _():
            par = 1 - parity_ref[0]
            for cp in weight_copies(expert, par):
                cp.wait()
            wgu_bf_ref[par, :, 0:D_EXPERT] = wg_st_ref[par].astype(BF16)
            wgu_bf_ref[par, :, D_EXPERT:] = wu_st_ref[par].astype(BF16)
            wd_bf_ref[par] = wd_st_ref[par].astype(BF16)
            next_first = meta_ref[META_PAD_END, expert] // FFN_BLOCK

            @pl.when(next_first < n_used)
            def _():
                for cp in weight_copies(meta_ref[META_BLOCK_E, next_first], 1 - par):
                    cp.start()
            parity_ref[0] = par
        buffers.append(parity_ref[0])

    for sub in range(FFN_BLOCKS_PER_STEP):
        row0 = sub * blk_rows
        x = jnp.concatenate([x_ref[pl.ds(row0 + c, FFN_BLOCK, stride=ROW_CHUNKS), :] for c in range(ROW_CHUNKS)],
                            axis=1).astype(BF16)
        gu = jnp.dot(x, wgu_bf_ref[buffers[sub]], preferred_element_type=F32)
        g, u = gu[:, 0:D_EXPERT], gu[:, D_EXPERT:]
        hid = (g * jax.nn.sigmoid(g) * u).astype(BF16)
        y = jnp.dot(hid, wd_bf_ref[buffers[sub]], preferred_element_type=F32)
        for c in range(ROW_CHUNKS):
            o_ref[pl.ds(row0 + c, FFN_BLOCK, stride=ROW_CHUNKS), :] = y[:, c * LANES:(c + 1) * LANES]


def _ffn(meta, xs, w_gate, w_up, w_down, n_blocks):
    assert n_blocks % FFN_BLOCKS_PER_STEP == 0
    blk_rows = FFN_BLOCK * ROW_CHUNKS
    row_spec = pl.BlockSpec((FFN_BLOCKS_PER_STEP * blk_rows, LANES), lambda i, m: (i, 0))
    return pl.pallas_call(
        _ffn_kernel,
        grid_spec=pltpu.PrefetchScalarGridSpec(
            num_scalar_prefetch=1,
            grid=(n_blocks // FFN_BLOCKS_PER_STEP,),
            in_specs=[
                row_spec,
                pl.BlockSpec(memory_space=pl.ANY),
                pl.BlockSpec(memory_space=pl.ANY),
                pl.BlockSpec(memory_space=pl.ANY),
            ],
            out_specs=row_spec,
            scratch_shapes=[
                pltpu.VMEM((2, D_MODEL, D_EXPERT), F32),
                pltpu.VMEM((2, D_MODEL, D_EXPERT), F32),
                pltpu.VMEM((2, D_EXPERT, D_MODEL), F32),
                pltpu.VMEM((2, D_MODEL, 2 * D_EXPERT), BF16),
                pltpu.VMEM((2, D_EXPERT, D_MODEL), BF16),
                pltpu.SMEM((1,), I32),
                pltpu.SemaphoreType.DMA((2, 3)),
            ],
        ),
        out_shape=jax.ShapeDtypeStruct(xs.shape, F32),
        compiler_params=_cparams(("arbitrary",)),
    )(meta, xs, w_gate, w_up, w_down)


def _combine_kernel(dest0_ref, dest1_ref, next_dest0_ref, next_dest1_ref, wts_ref, x1_ref, g2_ref, fg_ref, yb_hbm,
                    o_ref, buf_ref, sems):
    i = pl.program_id(0)
    last = pl.num_programs(0) - 1
    tt = x1_ref.shape[0]
    slot = i % 2

    def row_copy(buf_slot, k, t, row):
        src = pl.multiple_of(row * ROW_CHUNKS, ROW_CHUNKS)
        dst = pl.multiple_of(t * ROW_CHUNKS, ROW_CHUNKS)
        return pltpu.make_async_copy(yb_hbm.at[pl.ds(src, ROW_CHUNKS)],
                                     buf_ref.at[buf_slot, k, pl.ds(dst, ROW_CHUNKS)], sems.at[buf_slot])

    def issue_tile(idx_refs, buf_slot):
        def body(g, carry):
            for u in range(DMA_ISSUE_UNROLL):
                t = g * DMA_ISSUE_UNROLL + u
                for k in range(TOP_K):
                    row_copy(buf_slot, k, t, idx_refs[k][t]).start(priority=k)
            return carry
        lax.fori_loop(0, tt // DMA_ISSUE_UNROLL, body, 0)

    @pl.when(i == 0)
    def _():
        issue_tile((dest0_ref, dest1_ref), 0)

    @pl.when(i < last)
    def _():
        issue_tile((next_dest0_ref, next_dest1_ref), 1 - slot)

    def drain(g, carry):
        for _ in range(DMA_DRAIN_UNROLL):
            row_copy(slot, 0, 0, 0).wait()
        return carry
    lax.fori_loop(0, TOP_K * tt // DMA_DRAIN_UNROLL, drain, 0)

    eye = jnp.where(lax.broadcasted_iota(I32, (SUBLANES, LANES), 0) == lax.broadcasted_iota(I32, (SUBLANES, LANES), 1),
                    1.0, 0.0)
    wcol = lax.dot_general(wts_ref[...], eye, (((0,), (0,)), ((), ())), precision=HIGHEST,
                           preferred_element_type=F32)
    ya = jnp.concatenate([buf_ref[slot, 0, pl.ds(c, tt, stride=ROW_CHUNKS), :] for c in range(ROW_CHUNKS)], axis=1)
    yb = jnp.concatenate([buf_ref[slot, 1, pl.ds(c, tt, stride=ROW_CHUNKS), :] for c in range(ROW_CHUNKS)], axis=1)
    y = ya * wcol[:, 0:1] + yb * wcol[:, 1:2]
    x2 = x1_ref[...] + g2_ref[...] * y
    o_ref[...] = x2 * lax.rsqrt(jnp.mean(x2 * x2, axis=-1, keepdims=True) + RMS_EPS) * fg_ref[...]


def _combine(dests, wts, x1, mod4, final_g, yb, seq, tt=512):
    n = x1.shape[0]
    spt = seq // tt
    steps = n // tt
    idx_spec = pl.BlockSpec((tt,), lambda i: (i,), memory_space=pltpu.SMEM)
    next_idx_spec = pl.BlockSpec((tt,), lambda i: (jnp.minimum(i + 1, steps - 1),), memory_space=pltpu.SMEM)
    return pl.pallas_call(
        _combine_kernel,
        grid=(steps,),
        in_specs=[
            idx_spec, idx_spec, next_idx_spec, next_idx_spec,
            pl.BlockSpec((SUBLANES, tt), lambda i: (0, i)),
            pl.BlockSpec((tt, D_MODEL), lambda i: (i, 0)),
            pl.BlockSpec((None, None, 1, D_MODEL), lambda i: (5, i // spt, 0, 0)),
            pl.BlockSpec((1, D_MODEL), lambda i: (0, 0)),
            pl.BlockSpec(memory_space=pl.ANY),
        ],
        out_specs=pl.BlockSpec((tt, D_MODEL), lambda i: (i, 0)),
        out_shape=jax.ShapeDtypeStruct((n, D_MODEL), F32),
        scratch_shapes=[pltpu.VMEM((2, 2, tt * ROW_CHUNKS, LANES), F32), pltpu.SemaphoreType.DMA((2,))],
        compiler_params=_cparams(("arbitrary",)),
    )(*dests, *dests, wts, x1, mod4, final_g, yb)


def kernel(x, c, w_ada, b_ada, norm1_g, w_in, lambda_q1, lambda_k1, lambda_q2, lambda_k2, subln_g, w_pool,
           pool_scale, w_out, norm2_g, w_router_group, b_router_group, w_router_expert, b_router_expert,
           w_gate, w_up, w_down, final_g):
    bsz, seq, d = x.shape
    n = bsz * seq
    x2 = x.reshape(n, d)

    mod = _ada(c, w_ada[0], b_ada[0])
    mod4 = mod.reshape(mod.shape[0], bsz, 1, d)

    q, k, v, u = _inproj(x2, mod4, norm1_g, w_in[0].astype(BF16), seq)
    a_out = _attention(q, k, v, lambda_q1, lambda_k1, lambda_q2, lambda_k2, subln_g, bsz, seq)

    pad_rows = ROUTER_ROWS - N_EXPERTS - N_GROUPS
    wr = jnp.concatenate([w_router_expert[0].T, w_router_group[0].T, jnp.zeros((pad_rows, d), F32)], axis=0)
    wr_hi = wr.astype(BF16)
    wr_lo = (wr - wr_hi.astype(F32)).astype(BF16)
    wr = jnp.concatenate([wr_hi, wr_lo], axis=0)
    br = jnp.concatenate([b_router_expert[0], b_router_group[0], jnp.zeros((pad_rows,), F32)])
    br = jnp.broadcast_to(br[:, None], (br.shape[0], LANES))
    x1, h2rows, eid, wts = _mix(a_out, u, x2, mod4, w_pool[0].astype(BF16),
                                pool_scale[0].reshape(len(POOL_WINDOWS), 1, POOL_GROUP_DIM),
                                w_out[0].astype(BF16), norm2_g, wr, br, bsz, seq)

    n_blocks = pl.cdiv(2 * n, FFN_BLOCK) + N_EXPERTS
    dest, meta = _plan(eid, n_blocks)
    dests = tuple(dest[k] for k in range(TOP_K))
    xs = _dispatch(meta, dests, h2rows, n_blocks)
    yb = _ffn(meta, xs, w_gate[0], w_up[0], w_down[0], n_blocks)
    out = _combine(dests, wts, x1, mod4, final_g.reshape(1, d), yb, seq)
    return out.reshape(bsz, seq, d)
```

```python
import functools
import math

import jax
import jax.numpy as jnp
from jax import lax
from jax.experimental import pallas as pl
from jax.experimental.pallas import tpu as pltpu

F32 = jnp.float32
BF16 = jnp.bfloat16
I32 = jnp.int32
HIGHEST = lax.Precision.HIGHEST

D_MODEL = 1024
ATTN_WIDTH = 512
N_HEADS = 4
HEAD_DIM = 64
VALUE_DIM = 128
ROT_DIM = 16
ROPE_THETA = 500000.0
POOL_WINDOWS = (2, 4, 8, 16)
POOL_GROUP_DIM = 128
N_GROUPS = 4
EXPERTS_PER_GROUP = 8
N_EXPERTS = 32
TOP_K = 2
D_EXPERT = 512
RMS_EPS = 1e-6
LAMBDA_INIT = 0.8 - 0.6 * math.exp(-0.3 * 0)

LANES = 128
SUBLANES = 8
ROW_CHUNKS = D_MODEL // LANES

POOL_HALO = 8
FFN_BLOCK = 256
FFN_BLOCKS_PER_STEP = 2
DMA_ISSUE_UNROLL = 8
DMA_DRAIN_UNROLL = 16
ROUTER_ROWS = 48
VMEM_LIMIT = 56 * 1024 * 1024


def _cparams(sem, vmem=VMEM_LIMIT):
    return pltpu.CompilerParams(dimension_semantics=sem, vmem_limit_bytes=vmem)


def _ada_kernel(c_ref, w_ref, b_ref, o_ref):
    c = c_ref[...]
    act = c * jax.nn.sigmoid(c)
    o_ref[...] = jnp.dot(act, w_ref[...], precision=HIGHEST, preferred_element_type=F32) + b_ref[...]


def _ada(c, w_ada, b_ada):
    bsz = c.shape[0]
    n_mod = w_ada.shape[1] // D_MODEL
    return pl.pallas_call(
        _ada_kernel,
        grid=(n_mod,),
        in_specs=[
            pl.BlockSpec((bsz, D_MODEL), lambda j: (0, 0)),
            pl.BlockSpec((D_MODEL, D_MODEL), lambda j: (0, j)),
            pl.BlockSpec((1, D_MODEL), lambda j: (0, j)),
        ],
        out_specs=pl.BlockSpec((None, bsz, D_MODEL), lambda j: (j, 0, 0)),
        out_shape=jax.ShapeDtypeStruct((n_mod, bsz, D_MODEL), F32),
        compiler_params=_cparams(("arbitrary",)),
    )(c, w_ada, b_ada.reshape(1, -1))


def _inproj_kernel(x_ref, sh_ref, sc_ref, g_ref, w_ref, cos_ref, sa_ref, sb_ref,
                   q_ref, k_ref, v_ref, u_ref):
    x = x_ref[...]
    y = x * lax.rsqrt(jnp.mean(x * x, axis=-1, keepdims=True) + RMS_EPS) * g_ref[...]
    h = (y * (1.0 + sc_ref[...]) + sh_ref[...]).astype(BF16)
    cos, sa, sb = cos_ref[...], sa_ref[...], sb_ref[...]
    for part, o_ref, scale in ((0, q_ref, HEAD_DIM ** -0.5 * math.log2(math.e)), (1, k_ref, None)):
        z = jnp.dot(h, w_ref[:, part * ATTN_WIDTH:(part + 1) * ATTN_WIDTH], preferred_element_type=F32)
        for hd in range(N_HEADS):
            zc = z[:, hd * LANES:(hd + 1) * LANES]
            r = zc * cos + pltpu.roll(zc, ROT_DIM // 2, 1) * sa + pltpu.roll(zc, LANES - ROT_DIM // 2, 1) * sb
            if scale is not None:
                r = r * scale
            o_ref[:, hd * LANES:(hd + 1) * LANES] = r.astype(BF16)
    v_ref[...] = jnp.dot(h, w_ref[:, 2 * ATTN_WIDTH:3 * ATTN_WIDTH], preferred_element_type=F32).astype(BF16)
    u_ref[...] = jnp.dot(h, w_ref[:, 3 * ATTN_WIDTH:], preferred_element_type=F32).astype(BF16)


def _rope_lane_tables(seq):
    pos = jnp.arange(seq, dtype=F32)
    inv_freq = ROPE_THETA ** (-jnp.arange(0, ROT_DIM, 2, dtype=F32) / ROT_DIM)
    ang = pos[:, None] * inv_freq[None, :]
    cos, sin = jnp.cos(ang), jnp.sin(ang)
    half = ROT_DIM // 2
    ones = jnp.ones((seq, HEAD_DIM - ROT_DIM), F32)
    zeros_h = jnp.zeros((seq, half), F32)
    zeros_r = jnp.zeros((seq, HEAD_DIM - ROT_DIM), F32)
    cos_t = jnp.concatenate([cos, cos, ones], axis=1)
    sa_t = jnp.concatenate([zeros_h, sin, zeros_r], axis=1)
    sb_t = jnp.concatenate([-sin, zeros_h, zeros_r], axis=1)
    rep = LANES // HEAD_DIM
    return jnp.tile(cos_t, (1, rep)), jnp.tile(sa_t, (1, rep)), jnp.tile(sb_t, (1, rep))


def _inproj(x2, mod4, norm1_g, w_in_bf, seq, tm=1024):
    n = x2.shape[0]
    spt = seq // tm
    cos_t, sa_t, sb_t = _rope_lane_tables(seq)
    mod_spec = lambda which: pl.BlockSpec((None, None, 1, D_MODEL), lambda i: (which, i // spt, 0, 0))
    tab_spec = pl.BlockSpec((tm, LANES), lambda i: (i % spt, 0))
    out_spec = pl.BlockSpec((tm, ATTN_WIDTH), lambda i: (i, 0))
    out_sds = jax.ShapeDtypeStruct((n, ATTN_WIDTH), BF16)
    return pl.pallas_call(
        _inproj_kernel,
        grid=(n // tm,),
        in_specs=[
            pl.BlockSpec((tm, D_MODEL), lambda i: (i, 0)),
            mod_spec(0), mod_spec(1),
            pl.BlockSpec((1, D_MODEL), lambda i: (0, 0)),
            pl.BlockSpec((D_MODEL, 4 * ATTN_WIDTH), lambda i: (0, 0)),
            tab_spec, tab_spec, tab_spec,
        ],
        out_specs=[out_spec] * 4,
        out_shape=[out_sds] * 4,
        compiler_params=_cparams(("arbitrary",)),
    )(x2, mod4, mod4, norm1_g, w_in_bf, cos_t, sa_t, sb_t)


def _attn_kernel(lq1_ref, lk1_ref, lq2_ref, lk2_ref, sg_ref, q_ref, k_ref, v_ref, o_ref,
                 sa_ref, sb_ref, ma_ref, mb_ref, *, tq):
    lam = (jnp.exp(jnp.sum(lq1_ref[...] * lk1_ref[...], axis=-1, keepdims=True))
           - jnp.exp(jnp.sum(lq2_ref[...] * lk2_ref[...], axis=-1, keepdims=True)) + LAMBDA_INIT)
    n_tiles = q_ref.shape[0] // tq
    bufs = ((sa_ref, ma_ref), (sb_ref, mb_ref))

    def score_stage(j, s_ref, m_ref):
        q = q_ref[j * tq:(j + 1) * tq, :]
        lane = lax.broadcasted_iota(I32, q.shape, 1)
        zero = jnp.zeros_like(q)
        qq = jnp.concatenate([jnp.where(lane < HEAD_DIM, q, zero), jnp.where(lane >= HEAD_DIM, q, zero)], axis=0)
        s = lax.dot_general(qq, k_ref[...], (((1,), (1,)), ((), ())), preferred_element_type=F32)
        s_ref[...] = s
        m_ref[...] = jnp.max(s, axis=-1, keepdims=True)

    def value_stage(j, s_ref, m_ref):
        p = jnp.exp2(s_ref[...] - m_ref[...])
        l = jnp.sum(p, axis=-1, keepdims=True)
        pb = p.astype(BF16)
        ratio = (lam * l[:tq] / l[tq:]).astype(BF16)
        o = jnp.dot(pb[:tq] - pb[tq:] * ratio, v_ref[...], preferred_element_type=F32) / l[:tq]
        o = o * lax.rsqrt(jnp.mean(o * o, axis=-1, keepdims=True) + RMS_EPS) * sg_ref[...]
        o_ref[j * tq:(j + 1) * tq, :] = (o * (1.0 - LAMBDA_INIT)).astype(BF16)

    score_stage(0, *bufs[0])
    for j in range(n_tiles):
        if j + 1 < n_tiles:
            score_stage(j + 1, *bufs[(j + 1) % 2])
        value_stage(j, *bufs[j % 2])


def _attention(q, k, v, lq1, lk1, lq2, lk2, subln_g, bsz, seq, tq=256):
    n = q.shape[0]
    vec_spec = lambda width: pl.BlockSpec((1, width), lambda b, h: (0, 0))
    head_spec = pl.BlockSpec((seq, LANES), lambda b, h: (b, h))
    return pl.pallas_call(
        functools.partial(_attn_kernel, tq=tq),
        grid=(bsz, N_HEADS),
        in_specs=[vec_spec(HEAD_DIM)] * 4 + [vec_spec(VALUE_DIM), head_spec, head_spec, head_spec],
        out_specs=head_spec,
        out_shape=jax.ShapeDtypeStruct((n, ATTN_WIDTH), BF16),
        scratch_shapes=[pltpu.VMEM((2 * tq, seq), F32), pltpu.VMEM((2 * tq, seq), F32),
                        pltpu.VMEM((2 * tq, 1), F32), pltpu.VMEM((2 * tq, 1), F32)],
        compiler_params=_cparams(("arbitrary",) * 2),
    )(lq1, lk1, lq2, lk2, subln_g, q, k, v)


def _mix_kernel(a_ref, u_ref, x_ref, g1_ref, sh2_ref, sc2_ref, wp_ref, ps_ref, wo_ref, n2_ref,
                wr_ref, br_ref, x1_ref, h2_ref, eid_ref, wts_ref, up_ref, *, seq):
    j = pl.program_id(1)
    tm = x_ref.shape[0]
    pool_w = u_ref.shape[1]

    @pl.when(j == 0)
    def _():
        up_ref[0:POOL_HALO, :] = jnp.zeros((POOL_HALO, pool_w), F32)
        up_ref[POOL_HALO + seq:, :] = jnp.zeros((POOL_HALO, pool_w), F32)
        up_ref[POOL_HALO:POOL_HALO + seq, :] = u_ref[...].astype(F32)

    r0 = pl.multiple_of(j * tm, tm)
    pos = r0 + lax.broadcasted_iota(I32, (tm, 1), 0)
    heads = [a_ref[...]]
    for g, win in enumerate(POOL_WINDOWS):
        left = win // 2
        right = win - 1 - left
        cols = slice(g * POOL_GROUP_DIM, (g + 1) * POOL_GROUP_DIM)
        win_rows = up_ref[pl.ds(r0, tm + 2 * POOL_HALO), cols]
        tot, span = win_rows, 1
        while span < win:
            tot = tot[:tot.shape[0] - span] + tot[span:]
            span *= 2
        tot = tot[POOL_HALO - left:POOL_HALO - left + tm]
        cnt = (jnp.minimum(pos + (right + 1), seq) - jnp.maximum(pos - left, 0)).astype(F32)
        d = tot / cnt - win_rows[POOL_HALO:POOL_HALO + tm]
        yg = jnp.dot(d.astype(BF16), wp_ref[g], preferred_element_type=F32) * ps_ref[g]
        heads.append(yg.astype(BF16))
    mix = jnp.dot(jnp.concatenate(heads, axis=1), wo_ref[...], preferred_element_type=F32)
    x1 = x_ref[...] + g1_ref[...] * mix
    x1_ref[...] = x1
    y = x1 * lax.rsqrt(jnp.mean(x1 * x1, axis=-1, keepdims=True) + RMS_EPS) * n2_ref[...]
    h2 = y * (1.0 + sc2_ref[...]) + sh2_ref[...]
    for c in range(ROW_CHUNKS):
        h2_ref[pl.ds(c, tm, stride=ROW_CHUNKS), :] = h2[:, c * LANES:(c + 1) * LANES]

    n_rt = br_ref.shape[0]
    nt_dims = (((1,), (1,)), ((), ()))
    h_hi = h2.astype(BF16)
    h_lo = (h2 - h_hi.astype(F32)).astype(BF16)
    lt2 = lax.dot_general(wr_ref[...], h_hi, nt_dims, preferred_element_type=F32)
    lt = (lt2[:n_rt] + lt2[n_rt:] + lax.dot_general(wr_ref[0:n_rt, :], h_lo, nt_dims, preferred_element_type=F32)
          + br_ref[:, 0:1])
    gl = [lt[N_EXPERTS + i:N_EXPERTS + i + 1, :] for i in range(N_GROUPS)]
    gmax = jnp.maximum(jnp.maximum(gl[0], gl[1]), jnp.maximum(gl[2], gl[3]))
    gsel = jnp.where(gl[0] == gmax, 0, jnp.where(gl[1] == gmax, 1, jnp.where(gl[2] == gmax, 2, 3)))
    gden = sum(jnp.exp(g_ - gmax) for g_ in gl)
    e_in = lt[3 * EXPERTS_PER_GROUP:4 * EXPERTS_PER_GROUP, :]
    for i in (2, 1, 0):
        e_in = jnp.where(gsel == i, lt[i * EXPERTS_PER_GROUP:(i + 1) * EXPERTS_PER_GROUP, :], e_in)
    sub = lax.broadcasted_iota(I32, e_in.shape, 0)
    m1 = jnp.max(e_in, axis=0, keepdims=True)
    i1 = jnp.min(jnp.where(e_in == m1, sub, EXPERTS_PER_GROUP), axis=0, keepdims=True)
    rest = jnp.where(sub == i1, -jnp.inf, e_in)
    m2 = jnp.max(rest, axis=0, keepdims=True)
    i2 = jnp.min(jnp.where(rest == m2, sub, EXPERTS_PER_GROUP), axis=0, keepdims=True)
    p2 = jnp.exp(m2 - m1)
    w1 = 1.0 / ((1.0 + p2) * gden)
    w2 = p2 * w1
    row = lax.broadcasted_iota(I32, (SUBLANES, tm), 0)
    base = gsel * EXPERTS_PER_GROUP
    eid_ref[...] = jnp.where(row == 0, base + i1, jnp.where(row == 1, base + i2, 0))
    wts_ref[...] = jnp.where(row == 0, w1, jnp.where(row == 1, w2, 0.0))


def _mix(a_out, u, x2, mod4, w_pool_bf, pool_scale, w_out_bf, norm2_g, wr, br, bsz, seq, tm=512):
    n = x2.shape[0]
    spt = seq // tm
    mod_spec = lambda which: pl.BlockSpec((None, None, 1, D_MODEL), lambda b, j: (which, b, 0, 0))
    tok_spec = lambda width: pl.BlockSpec((tm, width), lambda b, j: (b * spt + j, 0))
    lane_spec = pl.BlockSpec((SUBLANES, tm), lambda b, j: (0, b * spt + j))
    return pl.pallas_call(
        functools.partial(_mix_kernel, seq=seq),
        grid=(bsz, spt),
        in_specs=[
            tok_spec(ATTN_WIDTH),
            pl.BlockSpec((seq, ATTN_WIDTH), lambda b, j: (b, 0)),
            tok_spec(D_MODEL),
            mod_spec(2), mod_spec(3), mod_spec(4),
            pl.BlockSpec((len(POOL_WINDOWS), POOL_GROUP_DIM, POOL_GROUP_DIM), lambda b, j: (0, 0, 0)),
            pl.BlockSpec((len(POOL_WINDOWS), 1, POOL_GROUP_DIM), lambda b, j: (0, 0, 0)),
            pl.BlockSpec((D_MODEL, D_MODEL), lambda b, j: (0, 0)),
            pl.BlockSpec((1, D_MODEL), lambda b, j: (0, 0)),
            pl.BlockSpec((2 * ROUTER_ROWS, D_MODEL), lambda b, j: (0, 0)),
            pl.BlockSpec((ROUTER_ROWS, LANES), lambda b, j: (0, 0)),
        ],
        out_specs=[
            tok_spec(D_MODEL),
            pl.BlockSpec((tm * ROW_CHUNKS, LANES), lambda b, j: (b * spt + j, 0)),
            lane_spec, lane_spec,
        ],
        out_shape=[
            jax.ShapeDtypeStruct((n, D_MODEL), F32),
            jax.ShapeDtypeStruct((n * ROW_CHUNKS, LANES), F32),
            jax.ShapeDtypeStruct((SUBLANES, n), I32),
            jax.ShapeDtypeStruct((SUBLANES, n), F32),
        ],
        scratch_shapes=[pltpu.VMEM((seq + 2 * POOL_HALO, ATTN_WIDTH), F32)],
        compiler_params=_cparams(("arbitrary", "arbitrary")),
    )(a_out, u, x2, mod4, mod4, mod4, w_pool_bf, pool_scale, w_out_bf, norm2_g, wr, br)


META_BLOCK_E, META_N_USED, META_PAD_END, META_COUNT = 0, 1, 2, 3


def _plan_kernel(eid_ref, dest_ref, meta_ref, cnt_ref, base_ref, *, n_blocks, meta_w, chunk):
    phase = pl.program_id(0)
    t = pl.program_id(1)
    nt = pl.num_programs(1)
    n_chunks = eid_ref.shape[1] // chunk
    sub = lax.broadcasted_iota(I32, (N_EXPERTS, chunk), 0)

    def one_hots(c):
        oh0 = sub == eid_ref[0:1, c * chunk:(c + 1) * chunk]
        oh1 = sub == eid_ref[1:2, c * chunk:(c + 1) * chunk]
        return oh0, oh1, jnp.where(oh0 | oh1, 1.0, 0.0)

    @pl.when((phase == 0) & (t == 0))
    def _():
        cnt_ref[...] = jnp.zeros_like(cnt_ref)

    @pl.when(phase == 0)
    def _():
        cnt_ref[...] += sum(jnp.sum(one_hots(c)[2], axis=1, keepdims=True) for c in range(n_chunks))

    @pl.when((phase == 0) & (t == nt - 1))
    def _():
        cnt = cnt_ref[...]
        padded = jnp.floor((cnt + (FFN_BLOCK - 1)) * (1.0 / FFN_BLOCK)) * FFN_BLOCK
        er = lax.broadcasted_iota(I32, (N_EXPERTS, N_EXPERTS), 0)
        ec = lax.broadcasted_iota(I32, (N_EXPERTS, N_EXPERTS), 1)
        tri = jnp.where(ec <= er, 1.0, 0.0)
        pad_end = jnp.dot(tri, padded, precision=HIGHEST, preferred_element_type=F32)
        base_ref[...] = pad_end - padded
        blk_row = (lax.broadcasted_iota(I32, (N_EXPERTS, meta_w), 1) * FFN_BLOCK).astype(F32)
        block_e = jnp.sum(jnp.where(pad_end[:, 0:1] <= blk_row, 1.0, 0.0), axis=0, keepdims=True)
        block_e = jnp.minimum(block_e, N_EXPERTS - 1.0)
        n_used = pad_end[N_EXPERTS - 1:N_EXPERTS, 0:1] * (1.0 / FFN_BLOCK)
        diag = (lax.broadcasted_iota(I32, (N_EXPERTS, LANES), 0)
                == lax.broadcasted_iota(I32, (N_EXPERTS, LANES), 1))
        pe_lane = jnp.sum(jnp.where(diag, pad_end, 0.0), axis=0, keepdims=True)
        cnt_lane = jnp.sum(jnp.where(diag, cnt, 0.0), axis=0, keepdims=True)
        zpad = jnp.zeros((1, meta_w - LANES), F32)
        row = lax.broadcasted_iota(I32, (SUBLANES, meta_w), 0)
        meta = jnp.where(row == META_BLOCK_E, block_e,
                         jnp.where(row == META_N_USED, n_used,
                                   jnp.where(row == META_PAD_END, jnp.concatenate([pe_lane, zpad], axis=1),
                                             jnp.where(row == META_COUNT, jnp.concatenate([cnt_lane, zpad], axis=1),
                                                       0.0))))
        meta_ref[...] = meta.astype(I32)

    @pl.when(phase == 1)
    def _():
        r = lax.broadcasted_iota(I32, (chunk, chunk), 0)
        c = lax.broadcasted_iota(I32, (chunk, chunk), 1)
        triu = jnp.where(r <= c, 1.0, 0.0).astype(BF16)
        row = lax.broadcasted_iota(I32, (SUBLANES, chunk), 0)
        base = base_ref[:, 0:1]
        for ci in range(n_chunks):
            oh0, oh1, oh = one_hots(ci)
            incl = jnp.dot(oh.astype(BF16), triu, preferred_element_type=F32)
            slot = base + incl - oh
            d0 = jnp.sum(jnp.where(oh0, slot, 0.0), axis=0, keepdims=True)
            d1 = jnp.sum(jnp.where(oh1, slot, 0.0), axis=0, keepdims=True)
            dest_ref[:, ci * chunk:(ci + 1) * chunk] = jnp.where(row == 0, d0, jnp.where(row == 1, d1, 0.0)).astype(I32)
            base = base + incl[:, chunk - 1:chunk]
        base_ref[...] = jnp.broadcast_to(base, base_ref.shape)


def _plan(eid, n_blocks, tt=2048, chunk=512):
    n = eid.shape[1]
    meta_w = pl.cdiv(n_blocks, LANES) * LANES
    return pl.pallas_call(
        functools.partial(_plan_kernel, n_blocks=n_blocks, meta_w=meta_w, chunk=chunk),
        grid=(2, n // tt),
        in_specs=[pl.BlockSpec((SUBLANES, tt), lambda p, t: (0, t))],
        out_specs=[
            pl.BlockSpec((SUBLANES, tt), lambda p, t: (0, p * t)),
            pl.BlockSpec((SUBLANES, meta_w), lambda p, t: (0, 0)),
        ],
        out_shape=[
            jax.ShapeDtypeStruct((SUBLANES, n), I32),
            jax.ShapeDtypeStruct((SUBLANES, meta_w), I32),
        ],
        scratch_shapes=[pltpu.VMEM((N_EXPERTS, LANES), F32), pltpu.VMEM((N_EXPERTS, LANES), F32)],
        compiler_params=_cparams(("arbitrary", "arbitrary")),
    )(eid)


def _dispatch_kernel(meta_ref, dest0_ref, dest1_ref, h2_ref, xs_hbm, zero_ref, zsem, sem, *, n_blocks):
    i = pl.program_id(0)
    tt = dest0_ref.shape[0]
    dest_refs = (dest0_ref, dest1_ref)
    blk_rows = FFN_BLOCK * ROW_CHUNKS

    def zero_copy(blk):
        start = pl.multiple_of(blk * blk_rows, blk_rows)
        return pltpu.make_async_copy(zero_ref, xs_hbm.at[pl.ds(start, blk_rows)], zsem)

    @pl.when(i == 0)
    def _():
        zero_ref[...] = jnp.zeros_like(zero_ref)
        for e in range(N_EXPERTS):
            @pl.when(meta_ref[META_COUNT, e] > 0)
            def _():
                blk = meta_ref[META_PAD_END, e] // FFN_BLOCK - 1
                cp = zero_copy(blk)
                cp.start()
                cp.wait()

        def tail(blk, carry):
            cp = zero_copy(blk)
            cp.start()
            cp.wait()
            return carry
        lax.fori_loop(meta_ref[META_N_USED, 0], n_blocks, tail, 0)

    def row_copy(t, slot):
        src = pl.multiple_of(t * ROW_CHUNKS, ROW_CHUNKS)
        dst = pl.multiple_of(slot * ROW_CHUNKS, ROW_CHUNKS)
        return pltpu.make_async_copy(h2_ref.at[pl.ds(src, ROW_CHUNKS)], xs_hbm.at[pl.ds(dst, ROW_CHUNKS)], sem)

    def issue(g, carry):
        for u in range(DMA_ISSUE_UNROLL):
            t = g * DMA_ISSUE_UNROLL + u
            for k in range(TOP_K):
                row_copy(t, dest_refs[k][t]).start(priority=k)
        return carry
    lax.fori_loop(0, tt // DMA_ISSUE_UNROLL, issue, 0)

    def drain(g, carry):
        for _ in range(DMA_DRAIN_UNROLL):
            row_copy(0, 0).wait()
        return carry
    lax.fori_loop(0, TOP_K * tt // DMA_DRAIN_UNROLL, drain, 0)


def _dispatch(meta, dests, h2rows, n_blocks, tt=2048):
    n = dests[0].shape[0]
    p_rows = n_blocks * FFN_BLOCK * ROW_CHUNKS
    idx_spec = pl.BlockSpec((tt,), lambda i, m: (i,), memory_space=pltpu.SMEM)
    return pl.pallas_call(
        functools.partial(_dispatch_kernel, n_blocks=n_blocks),
        grid_spec=pltpu.PrefetchScalarGridSpec(
            num_scalar_prefetch=1,
            grid=(n // tt,),
            in_specs=[
                idx_spec, idx_spec,
                pl.BlockSpec((tt * ROW_CHUNKS, LANES), lambda i, m: (i, 0)),
            ],
            out_specs=pl.BlockSpec(memory_space=pl.ANY),
            scratch_shapes=[
                pltpu.VMEM((FFN_BLOCK * ROW_CHUNKS, LANES), F32),
                pltpu.SemaphoreType.DMA,
                pltpu.SemaphoreType.DMA,
            ],
        ),
        out_shape=jax.ShapeDtypeStruct((p_rows, LANES), F32),
        compiler_params=_cparams(("arbitrary",)),
    )(meta, *dests, h2rows)


def _ffn_kernel(meta_ref, x_ref, wg_hbm, wu_hbm, wd_hbm, o_ref,
                wg_st_ref, wu_st_ref, wd_st_ref, wgu_bf_ref, wd_bf_ref, parity_ref, sems):
    n_used = meta_ref[META_N_USED, 0]
    blk_rows = FFN_BLOCK * ROW_CHUNKS

    def weight_copies(e, slot):
        return (pltpu.make_async_copy(wg_hbm.at[e], wg_st_ref.at[slot], sems.at[slot, 0]),
                pltpu.make_async_copy(wu_hbm.at[e], wu_st_ref.at[slot], sems.at[slot, 1]),
                pltpu.make_async_copy(wd_hbm.at[e], wd_st_ref.at[slot], sems.at[slot, 2]))

    @pl.when(pl.program_id(0) == 0)
    def _():
        parity_ref[0] = 1
        for cp in weight_copies(meta_ref[META_BLOCK_E, 0], 0):
            cp.start()

    buffers = []
    for sub in range(FFN_BLOCKS_PER_STEP):
        i = pl.program_id(0) * FFN_BLOCKS_PER_STEP + sub
        expert = meta_ref[META_BLOCK_E, i]
        prev_expert = meta_ref[META_BLOCK_E, jnp.maximum(i - 1, 0)]

        @pl.when((i < n_used) & ((i == 0) | (expert != prev_expert)))
        def _():
            par = 1 - parity_ref[0]
            for cp in weight_copies(expert, par):
                cp.wait()
            wgu_bf_ref[par, :, 0:D_EXPERT] = wg_st_ref[par].astype(BF16)
            wgu_bf_ref[par, :, D_EXPERT:] = wu_st_ref[par].astype(BF16)
            wd_bf_ref[par] = wd_st_ref[par].astype(BF16)
            next_first = meta_ref[META_PAD_END, expert] // FFN_BLOCK

            @pl.when(next_first < n_used)
            def _():
                for cp in weight_copies(meta_ref[META_BLOCK_E, next_first], 1 - par):
                    cp.start()
            parity_ref[0] = par
        buffers.append(parity_ref[0])

    for sub in range(FFN_BLOCKS_PER_STEP):
        row0 = sub * blk_rows
        x = jnp.concatenate([x_ref[pl.ds(row0 + c, FFN_BLOCK, stride=ROW_CHUNKS), :] for c in range(ROW_CHUNKS)],
                            axis=1).astype(BF16)
        gu = jnp.dot(x, wgu_bf_ref[buffers[sub]], preferred_element_type=F32)
        g, u = gu[:, 0:D_EXPERT], gu[:, D_EXPERT:]
        hid = (g * jax.nn.sigmoid(g) * u).astype(BF16)
        y = jnp.dot(hid, wd_bf_ref[buffers[sub]], preferred_element_type=F32)
        for c in range(ROW_CHUNKS):
            o_ref[pl.ds(row0 + c, FFN_BLOCK, stride=ROW_CHUNKS), :] = y[:, c * LANES:(c + 1) * LANES]


def _ffn(meta, xs, w_gate, w_up, w_down, n_blocks):
    assert n_blocks % FFN_BLOCKS_PER_STEP == 0
    blk_rows = FFN_BLOCK * ROW_CHUNKS
    row_spec = pl.BlockSpec((FFN_BLOCKS_PER_STEP * blk_rows, LANES), lambda i, m: (i, 0))
    return pl.pallas_call(
        _ffn_kernel,
        grid_spec=pltpu.PrefetchScalarGridSpec(
            num_scalar_prefetch=1,
            grid=(n_blocks // FFN_BLOCKS_PER_STEP,),
            in_specs=[
                row_spec,
                pl.BlockSpec(memory_space=pl.ANY),
                pl.BlockSpec(memory_space=pl.ANY),
                pl.BlockSpec(memory_space=pl.ANY),
            ],
            out_specs=row_spec,
            scratch_shapes=[
                pltpu.VMEM((2, D_MODEL, D_EXPERT), F32),
                pltpu.VMEM((2, D_MODEL, D_EXPERT), F32),
                pltpu.VMEM((2, D_EXPERT, D_MODEL), F32),
                pltpu.VMEM((2, D_MODEL, 2 * D_EXPERT), BF16),
                pltpu.VMEM((2, D_EXPERT, D_MODEL), BF16),
                pltpu.SMEM((1,), I32),
                pltpu.SemaphoreType.DMA((2, 3)),
            ],
        ),
        out_shape=jax.ShapeDtypeStruct(xs.shape, F32),
        compiler_params=_cparams(("arbitrary",)),
    )(meta, xs, w_gate, w_up, w_down)


def _combine_kernel(dest0_ref, dest1_ref, next_dest0_ref, next_dest1_ref, wts_ref, x1_ref, g2_ref, fg_ref, yb_hbm,
                    o_ref, buf_ref, sems):
    i = pl.program_id(0)
    last = pl.num_programs(0) - 1
    tt = x1_ref.shape[0]
    slot = i % 2

    def row_copy(buf_slot, k, t, row):
        src = pl.multiple_of(row * ROW_CHUNKS, ROW_CHUNKS)
        dst = pl.multiple_of(t * ROW_CHUNKS, ROW_CHUNKS)
        return pltpu.make_async_copy(yb_hbm.at[pl.ds(src, ROW_CHUNKS)],
                                     buf_ref.at[buf_slot, k, pl.ds(dst, ROW_CHUNKS)], sems.at[buf_slot])

    def issue_tile(idx_refs, buf_slot):
        def body(g, carry):
            for u in range(DMA_ISSUE_UNROLL):
                t = g * DMA_ISSUE_UNROLL + u
                for k in range(TOP_K):
                    row_copy(buf_slot, k, t, idx_refs[k][t]).start(priority=k)
            return carry
        lax.fori_loop(0, tt // DMA_ISSUE_UNROLL, body, 0)

    @pl.when(i == 0)
    def _():
        issue_tile((dest0_ref, dest1_ref), 0)

    @pl.when(i < last)
    def _():
        issue_tile((next_dest0_ref, next_dest1_ref), 1 - slot)

    def drain(g, carry):
        for _ in range(DMA_DRAIN_UNROLL):
            row_copy(slot, 0, 0, 0).wait()
        return carry
    lax.fori_loop(0, TOP_K * tt // DMA_DRAIN_UNROLL, drain, 0)

    eye = jnp.where(lax.broadcasted_iota(I32, (SUBLANES, LANES), 0) == lax.broadcasted_iota(I32, (SUBLANES, LANES), 1),
                    1.0, 0.0)
    wcol = lax.dot_general(wts_ref[...], eye, (((0,), (0,)), ((), ())), precision=HIGHEST,
                           preferred_element_type=F32)
    ya = jnp.concatenate([buf_ref[slot, 0, pl.ds(c, tt, stride=ROW_CHUNKS), :] for c in range(ROW_CHUNKS)], axis=1)
    yb = jnp.concatenate([buf_ref[slot, 1, pl.ds(c, tt, stride=ROW_CHUNKS), :] for c in range(ROW_CHUNKS)], axis=1)
    y = ya * wcol[:, 0:1] + yb * wcol[:, 1:2]
    x2 = x1_ref[...] + g2_ref[...] * y
    o_ref[...] = x2 * lax.rsqrt(jnp.mean(x2 * x2, axis=-1, keepdims=True) + RMS_EPS) * fg_ref[...]


def _combine(dests, wts, x1, mod4, final_g, yb, seq, tt=1024):
    n = x1.shape[0]
    spt = seq // tt
    steps = n // tt
    idx_spec = pl.BlockSpec((tt,), lambda i: (i,), memory_space=pltpu.SMEM)
    next_idx_spec = pl.BlockSpec((tt,), lambda i: (jnp.minimum(i + 1, steps - 1),), memory_space=pltpu.SMEM)
    return pl.pallas_call(
        _combine_kernel,
        grid=(steps,),
        in_specs=[
            idx_spec, idx_spec, next_idx_spec, next_idx_spec,
            pl.BlockSpec((SUBLANES, tt), lambda i: (0, i)),
            pl.BlockSpec((tt, D_MODEL), lambda i: (i, 0)),
            pl.BlockSpec((None, None, 1, D_MODEL), lambda i: (5, i // spt, 0, 0)),
            pl.BlockSpec((1, D_MODEL), lambda i: (0, 0)),
            pl.BlockSpec(memory_space=pl.ANY),
        ],
        out_specs=pl.BlockSpec((tt, D_MODEL), lambda i: (i, 0)),
        out_shape=jax.ShapeDtypeStruct((n, D_MODEL), F32),
        scratch_shapes=[pltpu.VMEM((2, 2, tt * ROW_CHUNKS, LANES), F32), pltpu.SemaphoreType.DMA((2,))],
        compiler_params=_cparams(("arbitrary",)),
    )(*dests, *dests, wts, x1, mod4, final_g, yb)


def kernel(x, c, w_ada, b_ada, norm1_g, w_in, lambda_q1, lambda_k1, lambda_q2, lambda_k2, subln_g, w_pool,
           pool_scale, w_out, norm2_g, w_router_group, b_router_group, w_router_expert, b_router_expert,
           w_gate, w_up, w_down, final_g):
    bsz, seq, d = x.shape
    n = bsz * seq
    x2 = x.reshape(n, d)

    mod = _ada(c, w_ada[0], b_ada[0])
    mod4 = mod.reshape(mod.shape[0], bsz, 1, d)

    q, k, v, u = _inproj(x2, mod4, norm1_g, w_in[0].astype(BF16), seq)
    a_out = _attention(q, k, v, lambda_q1, lambda_k1, lambda_q2, lambda_k2, subln_g, bsz, seq)

    pad_rows = ROUTER_ROWS - N_EXPERTS - N_GROUPS
    wr = jnp.concatenate([w_router_expert[0].T, w_router_group[0].T, jnp.zeros((pad_rows, d), F32)], axis=0)
    wr_hi = wr.astype(BF16)
    wr_lo = (wr - wr_hi.astype(F32)).astype(BF16)
    wr = jnp.concatenate([wr_hi, wr_lo], axis=0)
    br = jnp.concatenate([b_router_expert[0], b_router_group[0], jnp.zeros((pad_rows,), F32)])
    br = jnp.broadcast_to(br[:, None], (br.shape[0], LANES))
    x1, h2rows, eid, wts = _mix(a_out, u, x2, mod4, w_pool[0].astype(BF16),
                                pool_scale[0].reshape(len(POOL_WINDOWS), 1, POOL_GROUP_DIM),
                                w_out[0].astype(BF16), norm2_g, wr, br, bsz, seq)

    n_blocks = pl.cdiv(2 * n, FFN_BLOCK) + N_EXPERTS
    dest, meta = _plan(eid, n_blocks)
    dests = tuple(dest[k] for k in range(TOP_K))
    xs = _dispatch(meta, dests, h2rows, n_blocks)
    yb = _ffn(meta, xs, w_gate[0], w_up[0], w_down[0], n_blocks)
    out = _combine(dests, wts, x1, mod4, final_g.reshape(1, d), yb, seq)
    return out.reshape(bsz, seq, d)
```

```python
import functools
import math

import jax
import jax.numpy as jnp
from jax import lax
from jax.experimental import pallas as pl
from jax.experimental.pallas import tpu as pltpu

F32 = jnp.float32
BF16 = jnp.bfloat16
I32 = jnp.int32
HIGHEST = lax.Precision.HIGHEST

D_MODEL = 1024
ATTN_WIDTH = 512
N_HEADS = 4
HEAD_DIM = 64
VALUE_DIM = 128
ROT_DIM = 16
ROPE_THETA = 500000.0
POOL_WINDOWS = (2, 4, 8, 16)
POOL_GROUP_DIM = 128
N_GROUPS = 4
EXPERTS_PER_GROUP = 8
N_EXPERTS = 32
TOP_K = 2
D_EXPERT = 512
RMS_EPS = 1e-6
LAMBDA_INIT = 0.8 - 0.6 * math.exp(-0.3 * 0)

LANES = 128
SUBLANES = 8
ROW_CHUNKS = D_MODEL // LANES

POOL_HALO = 8
FFN_BLOCK = 256
FFN_BLOCKS_PER_STEP = 2
DMA_ISSUE_UNROLL = 8
DMA_DRAIN_UNROLL = 16
ROUTER_ROWS = 48
VMEM_LIMIT = 56 * 1024 * 1024


def _cparams(sem, vmem=VMEM_LIMIT):
    return pltpu.CompilerParams(dimension_semantics=sem, vmem_limit_bytes=vmem)


def _ada_kernel(c_ref, w_ref, b_ref, o_ref):
    c = c_ref[...]
    act = c * jax.nn.sigmoid(c)
    o_ref[...] = jnp.dot(act, w_ref[...], precision=HIGHEST, preferred_element_type=F32) + b_ref[...]


def _ada(c, w_ada, b_ada):
    bsz = c.shape[0]
    n_mod = w_ada.shape[1] // D_MODEL
    return pl.pallas_call(
        _ada_kernel,
        grid=(n_mod,),
        in_specs=[
            pl.BlockSpec((bsz, D_MODEL), lambda j: (0, 0)),
            pl.BlockSpec((D_MODEL, D_MODEL), lambda j: (0, j)),
            pl.BlockSpec((1, D_MODEL), lambda j: (0, j)),
        ],
        out_specs=pl.BlockSpec((None, bsz, D_MODEL), lambda j: (j, 0, 0)),
        out_shape=jax.ShapeDtypeStruct((n_mod, bsz, D_MODEL), F32),
        compiler_params=_cparams(("arbitrary",)),
    )(c, w_ada, b_ada.reshape(1, -1))


def _inproj_kernel(x_ref, sh_ref, sc_ref, g_ref, w_ref, cos_ref, sa_ref, sb_ref,
                   q_ref, k_ref, v_ref, u_ref):
    x = x_ref[...]
    y = x * lax.rsqrt(jnp.mean(x * x, axis=-1, keepdims=True) + RMS_EPS) * g_ref[...]
    h = (y * (1.0 + sc_ref[...]) + sh_ref[...]).astype(BF16)
    cos, sa, sb = cos_ref[...], sa_ref[...], sb_ref[...]
    for part, o_ref, scale in ((0, q_ref, HEAD_DIM ** -0.5 * math.log2(math.e)), (1, k_ref, None)):
        z = jnp.dot(h, w_ref[:, part * ATTN_WIDTH:(part + 1) * ATTN_WIDTH], preferred_element_type=F32)
        for hd in range(N_HEADS):
            zc = z[:, hd * LANES:(hd + 1) * LANES]
            r = zc * cos + pltpu.roll(zc, ROT_DIM // 2, 1) * sa + pltpu.roll(zc, LANES - ROT_DIM // 2, 1) * sb
            if scale is not None:
                r = r * scale
            o_ref[:, hd * LANES:(hd + 1) * LANES] = r.astype(BF16)
    v_ref[...] = jnp.dot(h, w_ref[:, 2 * ATTN_WIDTH:3 * ATTN_WIDTH], preferred_element_type=F32).astype(BF16)
    u_ref[...] = jnp.dot(h, w_ref[:, 3 * ATTN_WIDTH:], preferred_element_type=F32).astype(BF16)


def _rope_lane_tables(seq):
    pos = jnp.arange(seq, dtype=F32)
    inv_freq = ROPE_THETA ** (-jnp.arange(0, ROT_DIM, 2, dtype=F32) / ROT_DIM)
    ang = pos[:, None] * inv_freq[None, :]
    cos, sin = jnp.cos(ang), jnp.sin(ang)
    half = ROT_DIM // 2
    ones = jnp.ones((seq, HEAD_DIM - ROT_DIM), F32)
    zeros_h = jnp.zeros((seq, half), F32)
    zeros_r = jnp.zeros((seq, HEAD_DIM - ROT_DIM), F32)
    cos_t = jnp.concatenate([cos, cos, ones], axis=1)
    sa_t = jnp.concatenate([zeros_h, sin, zeros_r], axis=1)
    sb_t = jnp.concatenate([-sin, zeros_h, zeros_r], axis=1)
    rep = LANES // HEAD_DIM
    return jnp.tile(cos_t, (1, rep)), jnp.tile(sa_t, (1, rep)), jnp.tile(sb_t, (1, rep))


def _inproj(x2, mod4, norm1_g, w_in_bf, seq, tm=1024):
    n = x2.shape[0]
    spt = seq // tm
    cos_t, sa_t, sb_t = _rope_lane_tables(seq)
    mod_spec = lambda which: pl.BlockSpec((None, None, 1, D_MODEL), lambda i: (which, i // spt, 0, 0))
    tab_spec = pl.BlockSpec((tm, LANES), lambda i: (i % spt, 0))
    out_spec = pl.BlockSpec((tm, ATTN_WIDTH), lambda i: (i, 0))
    out_sds = jax.ShapeDtypeStruct((n, ATTN_WIDTH), BF16)
    return pl.pallas_call(
        _inproj_kernel,
        grid=(n // tm,),
        in_specs=[
            pl.BlockSpec((tm, D_MODEL), lambda i: (i, 0)),
            mod_spec(0), mod_spec(1),
            pl.BlockSpec((1, D_MODEL), lambda i: (0, 0)),
            pl.BlockSpec((D_MODEL, 4 * ATTN_WIDTH), lambda i: (0, 0)),
            tab_spec, tab_spec, tab_spec,
        ],
        out_specs=[out_spec] * 4,
        out_shape=[out_sds] * 4,
        compiler_params=_cparams(("arbitrary",)),
    )(x2, mod4, mod4, norm1_g, w_in_bf, cos_t, sa_t, sb_t)


def _attn_kernel(lq1_ref, lk1_ref, lq2_ref, lk2_ref, sg_ref, q_ref, k_ref, v_ref, o_ref,
                 sa_ref, sb_ref, ma_ref, mb_ref, *, tq):
    lam = (jnp.exp(jnp.sum(lq1_ref[...] * lk1_ref[...], axis=-1, keepdims=True))
           - jnp.exp(jnp.sum(lq2_ref[...] * lk2_ref[...], axis=-1, keepdims=True)) + LAMBDA_INIT)
    n_tiles = q_ref.shape[0] // tq
    bufs = ((sa_ref, ma_ref), (sb_ref, mb_ref))

    def score_stage(j, s_ref, m_ref):
        q = q_ref[j * tq:(j + 1) * tq, :]
        lane = lax.broadcasted_iota(I32, q.shape, 1)
        zero = jnp.zeros_like(q)
        qq = jnp.concatenate([jnp.where(lane < HEAD_DIM, q, zero), jnp.where(lane >= HEAD_DIM, q, zero)], axis=0)
        s = lax.dot_general(qq, k_ref[...], (((1,), (1,)), ((), ())), preferred_element_type=F32)
        s_ref[...] = s
        m_ref[...] = jnp.max(s, axis=-1, keepdims=True)

    def value_stage(j, s_ref, m_ref):
        p = jnp.exp2(s_ref[...] - m_ref[...])
        l = jnp.sum(p, axis=-1, keepdims=True)
        pb = p.astype(BF16)
        ratio = (lam * l[:tq] / l[tq:]).astype(BF16)
        o = jnp.dot(pb[:tq] - pb[tq:] * ratio, v_ref[...], preferred_element_type=F32) / l[:tq]
        o = o * lax.rsqrt(jnp.mean(o * o, axis=-1, keepdims=True) + RMS_EPS) * sg_ref[...]
        o_ref[j * tq:(j + 1) * tq, :] = (o * (1.0 - LAMBDA_INIT)).astype(BF16)

    score_stage(0, *bufs[0])
    for j in range(n_tiles):
        if j + 1 < n_tiles:
            score_stage(j + 1, *bufs[(j + 1) % 2])
        value_stage(j, *bufs[j % 2])


def _attention(q, k, v, lq1, lk1, lq2, lk2, subln_g, bsz, seq, tq=256):
    n = q.shape[0]
    vec_spec = lambda width: pl.BlockSpec((1, width), lambda b, h: (0, 0))
    head_spec = pl.BlockSpec((seq, LANES), lambda b, h: (b, h))
    return pl.pallas_call(
        functools.partial(_attn_kernel, tq=tq),
        grid=(bsz, N_HEADS),
        in_specs=[vec_spec(HEAD_DIM)] * 4 + [vec_spec(VALUE_DIM), head_spec, head_spec, head_spec],
        out_specs=head_spec,
        out_shape=jax.ShapeDtypeStruct((n, ATTN_WIDTH), BF16),
        scratch_shapes=[pltpu.VMEM((2 * tq, seq), F32), pltpu.VMEM((2 * tq, seq), F32),
                        pltpu.VMEM((2 * tq, 1), F32), pltpu.VMEM((2 * tq, 1), F32)],
        compiler_params=_cparams(("arbitrary",) * 2),
    )(lq1, lk1, lq2, lk2, subln_g, q, k, v)


def _mix_kernel(a_ref, u_ref, x_ref, g1_ref, sh2_ref, sc2_ref, wp_ref, ps_ref, wo_ref, n2_ref,
                wr_ref, br_ref, x1_ref, h2_ref, eid_ref, wts_ref, up_ref, *, seq):
    j = pl.program_id(1)
    tm = x_ref.shape[0]
    pool_w = u_ref.shape[1]

    @pl.when(j == 0)
    def _():
        up_ref[0:POOL_HALO, :] = jnp.zeros((POOL_HALO, pool_w), F32)
        up_ref[POOL_HALO + seq:, :] = jnp.zeros((POOL_HALO, pool_w), F32)
        up_ref[POOL_HALO:POOL_HALO + seq, :] = u_ref[...].astype(F32)

    r0 = pl.multiple_of(j * tm, tm)
    pos = r0 + lax.broadcasted_iota(I32, (tm, 1), 0)
    heads = [a_ref[...]]
    for g, win in enumerate(POOL_WINDOWS):
        left = win // 2
        right = win - 1 - left
        cols = slice(g * POOL_GROUP_DIM, (g + 1) * POOL_GROUP_DIM)
        win_rows = up_ref[pl.ds(r0, tm + 2 * POOL_HALO), cols]
        tot, span = win_rows, 1
        while span < win:
            tot = tot[:tot.shape[0] - span] + tot[span:]
            span *= 2
        tot = tot[POOL_HALO - left:POOL_HALO - left + tm]
        cnt = (jnp.minimum(pos + (right + 1), seq) - jnp.maximum(pos - left, 0)).astype(F32)
        d = tot / cnt - win_rows[POOL_HALO:POOL_HALO + tm]
        yg = jnp.dot(d.astype(BF16), wp_ref[g], preferred_element_type=F32) * ps_ref[g]
        heads.append(yg.astype(BF16))
    mix = jnp.dot(jnp.concatenate(heads, axis=1), wo_ref[...], preferred_element_type=F32)
    x1 = x_ref[...] + g1_ref[...] * mix
    x1_ref[...] = x1
    y = x1 * lax.rsqrt(jnp.mean(x1 * x1, axis=-1, keepdims=True) + RMS_EPS) * n2_ref[...]
    h2 = y * (1.0 + sc2_ref[...]) + sh2_ref[...]
    for c in range(ROW_CHUNKS):
        h2_ref[pl.ds(c, tm, stride=ROW_CHUNKS), :] = h2[:, c * LANES:(c + 1) * LANES]

    n_rt = br_ref.shape[0]
    nt_dims = (((1,), (1,)), ((), ()))
    h_hi = h2.astype(BF16)
    h_lo = (h2 - h_hi.astype(F32)).astype(BF16)
    lt2 = lax.dot_general(wr_ref[...], h_hi, nt_dims, preferred_element_type=F32)
    lt = (lt2[:n_rt] + lt2[n_rt:] + lax.dot_general(wr_ref[0:n_rt, :], h_lo, nt_dims, preferred_element_type=F32)
          + br_ref[:, 0:1])
    gl = [lt[N_EXPERTS + i:N_EXPERTS + i + 1, :] for i in range(N_GROUPS)]
    gmax = jnp.maximum(jnp.maximum(gl[0], gl[1]), jnp.maximum(gl[2], gl[3]))
    gsel = jnp.where(gl[0] == gmax, 0, jnp.where(gl[1] == gmax, 1, jnp.where(gl[2] == gmax, 2, 3)))
    gden = sum(jnp.exp(g_ - gmax) for g_ in gl)
    e_in = lt[3 * EXPERTS_PER_GROUP:4 * EXPERTS_PER_GROUP, :]
    for i in (2, 1, 0):
        e_in = jnp.where(gsel == i, lt[i * EXPERTS_PER_GROUP:(i + 1) * EXPERTS_PER_GROUP, :], e_in)
    sub = lax.broadcasted_iota(I32, e_in.shape, 0)
    m1 = jnp.max(e_in, axis=0, keepdims=True)
    i1 = jnp.min(jnp.where(e_in == m1, sub, EXPERTS_PER_GROUP), axis=0, keepdims=True)
    rest = jnp.where(sub == i1, -jnp.inf, e_in)
    m2 = jnp.max(rest, axis=0, keepdims=True)
    i2 = jnp.min(jnp.where(rest == m2, sub, EXPERTS_PER_GROUP), axis=0, keepdims=True)
    p2 = jnp.exp(m2 - m1)
    w1 = 1.0 / ((1.0 + p2) * gden)
    w2 = p2 * w1
    row = lax.broadcasted_iota(I32, (SUBLANES, tm), 0)
    base = gsel * EXPERTS_PER_GROUP
    eid_ref[...] = jnp.where(row == 0, base + i1, jnp.where(row == 1, base + i2, 0))
    wts_ref[...] = jnp.where(row == 0, w1, jnp.where(row == 1, w2, 0.0))


def _mix(a_out, u, x2, mod4, w_pool_bf, pool_scale, w_out_bf, norm2_g, wr, br, bsz, seq, tm=512):
    n = x2.shape[0]
    spt = seq // tm
    mod_spec = lambda which: pl.BlockSpec((None, None, 1, D_MODEL), lambda b, j: (which, b, 0, 0))
    tok_spec = lambda width: pl.BlockSpec((tm, width), lambda b, j: (b * spt + j, 0))
    lane_spec = pl.BlockSpec((SUBLANES, tm), lambda b, j: (0, b * spt + j))
    return pl.pallas_call(
        functools.partial(_mix_kernel, seq=seq),
        grid=(bsz, spt),
        in_specs=[
            tok_spec(ATTN_WIDTH),
            pl.BlockSpec((seq, ATTN_WIDTH), lambda b, j: (b, 0)),
            tok_spec(D_MODEL),
            mod_spec(2), mod_spec(3), mod_spec(4),
            pl.BlockSpec((len(POOL_WINDOWS), POOL_GROUP_DIM, POOL_GROUP_DIM), lambda b, j: (0, 0, 0)),
            pl.BlockSpec((len(POOL_WINDOWS), 1, POOL_GROUP_DIM), lambda b, j: (0, 0, 0)),
            pl.BlockSpec((D_MODEL, D_MODEL), lambda b, j: (0, 0)),
            pl.BlockSpec((1, D_MODEL), lambda b, j: (0, 0)),
            pl.BlockSpec((2 * ROUTER_ROWS, D_MODEL), lambda b, j: (0, 0)),
            pl.BlockSpec((ROUTER_ROWS, LANES), lambda b, j: (0, 0)),
        ],
        out_specs=[
            tok_spec(D_MODEL),
            pl.BlockSpec((tm * ROW_CHUNKS, LANES), lambda b, j: (b * spt + j, 0)),
            lane_spec, lane_spec,
        ],
        out_shape=[
            jax.ShapeDtypeStruct((n, D_MODEL), F32),
            jax.ShapeDtypeStruct((n * ROW_CHUNKS, LANES), F32),
            jax.ShapeDtypeStruct((SUBLANES, n), I32),
            jax.ShapeDtypeStruct((SUBLANES, n), F32),
        ],
        scratch_shapes=[pltpu.VMEM((seq + 2 * POOL_HALO, ATTN_WIDTH), F32)],
        compiler_params=_cparams(("arbitrary", "arbitrary")),
    )(a_out, u, x2, mod4, mod4, mod4, w_pool_bf, pool_scale, w_out_bf, norm2_g, wr, br)


META_BLOCK_E, META_N_USED, META_PAD_END, META_COUNT = 0, 1, 2, 3


def _plan_kernel(eid_ref, dest_ref, meta_ref, cnt_ref, base_ref, *, n_blocks, meta_w, chunk):
    phase = pl.program_id(0)
    t = pl.program_id(1)
    nt = pl.num_programs(1)
    n_chunks = eid_ref.shape[1] // chunk
    sub = lax.broadcasted_iota(I32, (N_EXPERTS, chunk), 0)

    def one_hots(c):
        oh0 = sub == eid_ref[0:1, c * chunk:(c + 1) * chunk]
        oh1 = sub == eid_ref[1:2, c * chunk:(c + 1) * chunk]
        return oh0, oh1, jnp.where(oh0 | oh1, 1.0, 0.0)

    @pl.when((phase == 0) & (t == 0))
    def _():
        cnt_ref[...] = jnp.zeros_like(cnt_ref)

    @pl.when(phase == 0)
    def _():
        cnt_ref[...] += sum(jnp.sum(one_hots(c)[2], axis=1, keepdims=True) for c in range(n_chunks))

    @pl.when((phase == 0) & (t == nt - 1))
    def _():
        cnt = cnt_ref[...]
        padded = jnp.floor((cnt + (FFN_BLOCK - 1)) * (1.0 / FFN_BLOCK)) * FFN_BLOCK
        er = lax.broadcasted_iota(I32, (N_EXPERTS, N_EXPERTS), 0)
        ec = lax.broadcasted_iota(I32, (N_EXPERTS, N_EXPERTS), 1)
        tri = jnp.where(ec <= er, 1.0, 0.0)
        pad_end = jnp.dot(tri, padded, precision=HIGHEST, preferred_element_type=F32)
        base_ref[...] = pad_end - padded
        blk_row = (lax.broadcasted_iota(I32, (N_EXPERTS, meta_w), 1) * FFN_BLOCK).astype(F32)
        block_e = jnp.sum(jnp.where(pad_end[:, 0:1] <= blk_row, 1.0, 0.0), axis=0, keepdims=True)
        block_e = jnp.minimum(block_e, N_EXPERTS - 1.0)
        n_used = pad_end[N_EXPERTS - 1:N_EXPERTS, 0:1] * (1.0 / FFN_BLOCK)
        diag = (lax.broadcasted_iota(I32, (N_EXPERTS, LANES), 0)
                == lax.broadcasted_iota(I32, (N_EXPERTS, LANES), 1))
        pe_lane = jnp.sum(jnp.where(diag, pad_end, 0.0), axis=0, keepdims=True)
        cnt_lane = jnp.sum(jnp.where(diag, cnt, 0.0), axis=0, keepdims=True)
        zpad = jnp.zeros((1, meta_w - LANES), F32)
        row = lax.broadcasted_iota(I32, (SUBLANES, meta_w), 0)
        meta = jnp.where(row == META_BLOCK_E, block_e,
                         jnp.where(row == META_N_USED, n_used,
                                   jnp.where(row == META_PAD_END, jnp.concatenate([pe_lane, zpad], axis=1),
                                             jnp.where(row == META_COUNT, jnp.concatenate([cnt_lane, zpad], axis=1),
                                                       0.0))))
        meta_ref[...] = meta.astype(I32)

    @pl.when(phase == 1)
    def _():
        r = lax.broadcasted_iota(I32, (chunk, chunk), 0)
        c = lax.broadcasted_iota(I32, (chunk, chunk), 1)
        triu = jnp.where(r <= c, 1.0, 0.0).astype(BF16)
        row = lax.broadcasted_iota(I32, (SUBLANES, chunk), 0)
        base = base_ref[:, 0:1]
        for ci in range(n_chunks):
            oh0, oh1, oh = one_hots(ci)
            incl = jnp.dot(oh.astype(BF16), triu, preferred_element_type=F32)
            slot = base + incl - oh
            d0 = jnp.sum(jnp.where(oh0, slot, 0.0), axis=0, keepdims=True)
            d1 = jnp.sum(jnp.where(oh1, slot, 0.0), axis=0, keepdims=True)
            dest_ref[:, ci * chunk:(ci + 1) * chunk] = jnp.where(row == 0, d0, jnp.where(row == 1, d1, 0.0)).astype(I32)
            base = base + incl[:, chunk - 1:chunk]
        base_ref[...] = jnp.broadcast_to(base, base_ref.shape)


def _plan(eid, n_blocks, tt=2048, chunk=512):
    n = eid.shape[1]
    meta_w = pl.cdiv(n_blocks, LANES) * LANES
    return pl.pallas_call(
        functools.partial(_plan_kernel, n_blocks=n_blocks, meta_w=meta_w, chunk=chunk),
        grid=(2, n // tt),
        in_specs=[pl.BlockSpec((SUBLANES, tt), lambda p, t: (0, t))],
        out_specs=[
            pl.BlockSpec((SUBLANES, tt), lambda p, t: (0, p * t)),
            pl.BlockSpec((SUBLANES, meta_w), lambda p, t: (0, 0)),
        ],
        out_shape=[
            jax.ShapeDtypeStruct((SUBLANES, n), I32),
            jax.ShapeDtypeStruct((SUBLANES, meta_w), I32),
        ],
        scratch_shapes=[pltpu.VMEM((N_EXPERTS, LANES), F32), pltpu.VMEM((N_EXPERTS, LANES), F32)],
        compiler_params=_cparams(("arbitrary", "arbitrary")),
    )(eid)


def _dispatch_kernel(meta_ref, dest0_ref, dest1_ref, h2_ref, xs_hbm, zero_ref, zsem, sem, *, n_blocks):
    i = pl.program_id(0)
    tt = dest0_ref.shape[0]
    dest_refs = (dest0_ref, dest1_ref)
    blk_rows = FFN_BLOCK * ROW_CHUNKS

    def zero_copy(blk):
        start = pl.multiple_of(blk * blk_rows, blk_rows)
        return pltpu.make_async_copy(zero_ref, xs_hbm.at[pl.ds(start, blk_rows)], zsem)

    @pl.when(i == 0)
    def _():
        zero_ref[...] = jnp.zeros_like(zero_ref)
        for e in range(N_EXPERTS):
            @pl.when(meta_ref[META_COUNT, e] > 0)
            def _():
                blk = meta_ref[META_PAD_END, e] // FFN_BLOCK - 1
                cp = zero_copy(blk)
                cp.start()
                cp.wait()

        def tail(blk, carry):
            cp = zero_copy(blk)
            cp.start()
            cp.wait()
            return carry
        lax.fori_loop(meta_ref[META_N_USED, 0], n_blocks, tail, 0)

    def row_copy(t, slot):
        src = pl.multiple_of(t * ROW_CHUNKS, ROW_CHUNKS)
        dst = pl.multiple_of(slot * ROW_CHUNKS, ROW_CHUNKS)
        return pltpu.make_async_copy(h2_ref.at[pl.ds(src, ROW_CHUNKS)], xs_hbm.at[pl.ds(dst, ROW_CHUNKS)], sem)

    def issue(g, carry):
        for u in range(DMA_ISSUE_UNROLL):
            t = g * DMA_ISSUE_UNROLL + u
            for k in range(TOP_K):
                row_copy(t, dest_refs[k][t]).start(priority=k)
        return carry
    lax.fori_loop(0, tt // DMA_ISSUE_UNROLL, issue, 0)

    def drain(g, carry):
        for _ in range(DMA_DRAIN_UNROLL):
            row_copy(0, 0).wait()
        return carry
    lax.fori_loop(0, TOP_K * tt // DMA_DRAIN_UNROLL, drain, 0)


def _dispatch(meta, dests, h2rows, n_blocks, tt=2048):
    n = dests[0].shape[0]
    p_rows = n_blocks * FFN_BLOCK * ROW_CHUNKS
    idx_spec = pl.BlockSpec((tt,), lambda i, m: (i,), memory_space=pltpu.SMEM)
    return pl.pallas_call(
        functools.partial(_dispatch_kernel, n_blocks=n_blocks),
        grid_spec=pltpu.PrefetchScalarGridSpec(
            num_scalar_prefetch=1,
            grid=(n // tt,),
            in_specs=[
                idx_spec, idx_spec,
                pl.BlockSpec((tt * ROW_CHUNKS, LANES), lambda i, m: (i, 0)),
            ],
            out_specs=pl.BlockSpec(memory_space=pl.ANY),
            scratch_shapes=[
                pltpu.VMEM((FFN_BLOCK * ROW_CHUNKS, LANES), F32),
                pltpu.SemaphoreType.DMA,
                pltpu.SemaphoreType.DMA,
            ],
        ),
        out_shape=jax.ShapeDtypeStruct((p_rows, LANES), F32),
        compiler_params=_cparams(("arbitrary",)),
    )(meta, *dests, h2rows)


def _ffn_kernel(meta_ref, x_ref, wg_hbm, wu_hbm, wd_hbm, o_ref,
                wg_st_ref, wu_st_ref, wd_st_ref, wgu_bf_ref, wd_bf_ref, parity_ref, sems):
    n_used = meta_ref[META_N_USED, 0]
    blk_rows = FFN_BLOCK * ROW_CHUNKS

    def weight_copies(e, slot):
        return (pltpu.make_async_copy(wg_hbm.at[e], wg_st_ref.at[slot], sems.at[slot, 0]),
                pltpu.make_async_copy(wu_hbm.at[e], wu_st_ref.at[slot], sems.at[slot, 1]),
                pltpu.make_async_copy(wd_hbm.at[e], wd_st_ref.at[slot], sems.at[slot, 2]))

    @pl.when(pl.program_id(0) == 0)
    def _():
        parity_ref[0] = 1
        for cp in weight_copies(meta_ref[META_BLOCK_E, 0], 0):
            cp.start()

    buffers = []
    for sub in range(FFN_BLOCKS_PER_STEP):
        i = pl.program_id(0) * FFN_BLOCKS_PER_STEP + sub
        expert = meta_ref[META_BLOCK_E, i]
        prev_expert = meta_ref[META_BLOCK_E, jnp.maximum(i - 1, 0)]

        @pl.when((i < n_used) & ((i == 0) | (expert != prev_expert)))
        def _():
            par = 1 - parity_ref[0]
            for cp in weight_copies(expert, par):
                cp.wait()
            wgu_bf_ref[par, :, 0:D_EXPERT] = wg_st_ref[par].astype(BF16)
            wgu_bf_ref[par, :, D_EXPERT:] = wu_st_ref[par].astype(BF16)
            wd_bf_ref[par] = wd_st_ref[par].astype(BF16)
            next_first = meta_ref[META_PAD_END, expert] // FFN_BLOCK

            @pl.when(next_first < n_used)
            def _():
                for cp in weight_copies(meta_ref[META_BLOCK_E, next_first], 1 - par):
                    cp.start()
            parity_ref[0] = par
        buffers.append(parity_ref[0])

    for sub in range(FFN_BLOCKS_PER_STEP):
        row0 = sub * blk_rows
        x = jnp.concatenate([x_ref[pl.ds(row0 + c, FFN_BLOCK, stride=ROW_CHUNKS), :] for c in range(ROW_CHUNKS)],
                            axis=1).astype(BF16)
        gu = jnp.dot(x, wgu_bf_ref[buffers[sub]], preferred_element_type=F32)
        g, u = gu[:, 0:D_EXPERT], gu[:, D_EXPERT:]
        hid = (g * jax.nn.sigmoid(g) * u).astype(BF16)
        y = jnp.dot(hid, wd_bf_ref[buffers[sub]], preferred_element_type=F32)
        for c in range(ROW_CHUNKS):
            o_ref[pl.ds(row0 + c, FFN_BLOCK, stride=ROW_CHUNKS), :] = y[:, c * LANES:(c + 1) * LANES]


def _ffn(meta, xs, w_gate, w_up, w_down, n_blocks):
    assert n_blocks % FFN_BLOCKS_PER_STEP == 0
    blk_rows = FFN_BLOCK * ROW_CHUNKS
    row_spec = pl.BlockSpec((FFN_BLOCKS_PER_STEP * blk_rows, LANES), lambda i, m: (i, 0))
    return pl.pallas_call(
        _ffn_kernel,
        grid_spec=pltpu.PrefetchScalarGridSpec(
            num_scalar_prefetch=1,
            grid=(n_blocks // FFN_BLOCKS_PER_STEP,),
            in_specs=[
                row_spec,
                pl.BlockSpec(memory_space=pl.ANY),
                pl.BlockSpec(memory_space=pl.ANY),
                pl.BlockSpec(memory_space=pl.ANY),
            ],
            out_specs=row_spec,
            scratch_shapes=[
                pltpu.VMEM((2, D_MODEL, D_EXPERT), F32),
                pltpu.VMEM((2, D_MODEL, D_EXPERT), F32),
                pltpu.VMEM((2, D_EXPERT, D_MODEL), F32),
                pltpu.VMEM((2, D_MODEL, 2 * D_EXPERT), BF16),
                pltpu.VMEM((2, D_EXPERT, D_MODEL), BF16),
                pltpu.SMEM((1,), I32),
                pltpu.SemaphoreType.DMA((2, 3)),
            ],
        ),
        out_shape=jax.ShapeDtypeStruct(xs.shape, F32),
        compiler_params=_cparams(("arbitrary",)),
    )(meta, xs, w_gate, w_up, w_down)


def _combine_kernel(dest0_ref, dest1_ref, next_dest0_ref, next_dest1_ref, wts_ref, x1_ref, g2_ref, fg_ref, yb_hbm,
                    o_ref, buf_ref, sems):
    i = pl.program_id(0)
    last = pl.num_programs(0) - 1
    tt = x1_ref.shape[0]
    slot = i % 2

    def row_copy(buf_slot, k, t, row):
        src = pl.multiple_of(row * ROW_CHUNKS, ROW_CHUNKS)
        dst = pl.multiple_of(t * ROW_CHUNKS, ROW_CHUNKS)
        return pltpu.make_async_copy(yb_hbm.at[pl.ds(src, ROW_CHUNKS)],
                                     buf_ref.at[buf_slot, k, pl.ds(dst, ROW_CHUNKS)], sems.at[buf_slot])

    def issue_tile(idx_refs, buf_slot):
        def body(g, carry):
            for u in range(DMA_ISSUE_UNROLL):
                t = g * DMA_ISSUE_UNROLL + u
                for k in range(TOP_K):
                    row_copy(buf_slot, k, t, idx_refs[k][t]).start(priority=k)
            return carry
        lax.fori_loop(0, tt // DMA_ISSUE_UNROLL, body, 0)

    @pl.when(i == 0)
    def _():
        issue_tile((dest0_ref, dest1_ref), 0)

    @pl.when(i < last)
    def _():
        issue_tile((next_dest0_ref, next_dest1_ref), 1 - slot)

    def drain(g, carry):
        for _ in range(DMA_DRAIN_UNROLL):
            row_copy(slot, 0, 0, 0).wait()
        return carry
    lax.fori_loop(0, TOP_K * tt // DMA_DRAIN_UNROLL, drain, 0)

    eye = jnp.where(lax.broadcasted_iota(I32, (SUBLANES, LANES), 0) == lax.broadcasted_iota(I32, (SUBLANES, LANES), 1),
                    1.0, 0.0)
    wcol = lax.dot_general(wts_ref[...], eye, (((0,), (0,)), ((), ())), precision=HIGHEST,
                           preferred_element_type=F32)
    ya = jnp.concatenate([buf_ref[slot, 0, pl.ds(c, tt, stride=ROW_CHUNKS), :] for c in range(ROW_CHUNKS)], axis=1)
    yb = jnp.concatenate([buf_ref[slot, 1, pl.ds(c, tt, stride=ROW_CHUNKS), :] for c in range(ROW_CHUNKS)], axis=1)
    y = ya * wcol[:, 0:1] + yb * wcol[:, 1:2]
    x2 = x1_ref[...] + g2_ref[...] * y
    o_ref[...] = x2 * lax.rsqrt(jnp.mean(x2 * x2, axis=-1, keepdims=True) + RMS_EPS) * fg_ref[...]


def _combine(dests, wts, x1, mod4, final_g, yb, seq, tt=512):
    n = x1.shape[0]
    spt = seq // tt
    steps = n // tt
    idx_spec = pl.BlockSpec((tt,), lambda i: (i,), memory_space=pltpu.SMEM)
    next_idx_spec = pl.BlockSpec((tt,), lambda i: (jnp.minimum(i + 1, steps - 1),), memory_space=pltpu.SMEM)
    return pl.pallas_call(
        _combine_kernel,
        grid=(steps,),
        in_specs=[
            idx_spec, idx_spec, next_idx_spec, next_idx_spec,
            pl.BlockSpec((SUBLANES, tt), lambda i: (0, i)),
            pl.BlockSpec((tt, D_MODEL), lambda i: (i, 0)),
            pl.BlockSpec((None, None, 1, D_MODEL), lambda i: (5, i // spt, 0, 0)),
            pl.BlockSpec((1, D_MODEL), lambda i: (0, 0)),
            pl.BlockSpec(memory_space=pl.ANY),
        ],
        out_specs=pl.BlockSpec((tt, D_MODEL), lambda i: (i, 0)),
        out_shape=jax.ShapeDtypeStruct((n, D_MODEL), F32),
        scratch_shapes=[pltpu.VMEM((2, 2, tt * ROW_CHUNKS, LANES), F32), pltpu.SemaphoreType.DMA((2,))],
        compiler_params=_cparams(("arbitrary",)),
    )(*dests, *dests, wts, x1, mod4, final_g, yb)


def kernel(x, c, w_ada, b_ada, norm1_g, w_in, lambda_q1, lambda_k1, lambda_q2, lambda_k2, subln_g, w_pool,
           pool_scale, w_out, norm2_g, w_router_group, b_router_group, w_router_expert, b_router_expert,
           w_gate, w_up, w_down, final_g):
    bsz, seq, d = x.shape
    n = bsz * seq
    x2 = x.reshape(n, d)

    mod = _ada(c, w_ada[0], b_ada[0])
    mod4 = mod.reshape(mod.shape[0], bsz, 1, d)

    q, k, v, u = _inproj(x2, mod4, norm1_g, w_in[0].astype(BF16), seq)
    a_out = _attention(q, k, v, lambda_q1, lambda_k1, lambda_q2, lambda_k2, subln_g, bsz, seq)

    pad_rows = ROUTER_ROWS - N_EXPERTS - N_GROUPS
    wr = jnp.concatenate([w_router_expert[0].T, w_router_group[0].T, jnp.zeros((pad_rows, d), F32)], axis=0)
    wr_hi = wr.astype(BF16)
    wr_lo = (wr - wr_hi.astype(F32)).astype(BF16)
    wr = jnp.concatenate([wr_hi, wr_lo], axis=0)
    br = jnp.concatenate([b_router_expert[0], b_router_group[0], jnp.zeros((pad_rows,), F32)])
    br = jnp.broadcast_to(br[:, None], (br.shape[0], LANES))
    x1, h2rows, eid, wts = _mix(a_out, u, x2, mod4, w_pool[0].astype(BF16),
                                pool_scale[0].reshape(len(POOL_WINDOWS), 1, POOL_GROUP_DIM),
                                w_out[0].astype(BF16), norm2_g, wr, br, bsz, seq)

    n_blocks = pl.cdiv(2 * n, FFN_BLOCK) + N_EXPERTS
    dest, meta = _plan(eid, n_blocks)
    dests = tuple(dest[k] for k in range(TOP_K))
    xs = _dispatch(meta, dests, h2rows, n_blocks)
    yb = _ffn(meta, xs, w_gate[0], w_up[0], w_down[0], n_blocks)
    out = _combine(dests, wts, x1, mod4, final_g.reshape(1, d), yb, seq)
    return out.reshape(bsz, seq, d)
```
